```python
import math
import jax, jax.numpy as jnp
from jax import lax
import numpy as np

D_MODEL = 1024
BATCH = 4
SEQ = 8192
DEPTH = 2

D_MIX = 1024
MLA_HEADS = 4
MLA_NOPE = 128
MLA_ROPE = 64
MLA_V = 128
MLA_QK = MLA_NOPE + MLA_ROPE
MLA_Q_LORA = 384
MLA_KV_LORA = 256
MLA_WIDTH = MLA_HEADS * MLA_V
ROPE_THETA = 10000.0
Q_BLOCK = 128
GLA_HEADS = 4
GLA_DK = 32
GLA_DV = 64
GLA_QK_WIDTH = GLA_HEADS * GLA_DK
GLA_WIDTH = GLA_HEADS * GLA_DV
GLA_GATE_RANK = 16
GLA_GATE_NORM = 16.0
GLA_CHUNK = 64
RWKV_HEADS = 4
RWKV_N = 64
RWKV_WIDTH = RWKV_HEADS * RWKV_N
RWKV_DECAY_RANK = 64
RWKV_A_RANK = 64
RWKV_GATE_RANK = 128
RWKV_LN_EPS = 64e-5
MLA_COLS = MLA_Q_LORA + MLA_KV_LORA + MLA_ROPE
GLA_COLS = 2 * GLA_QK_WIDTH + GLA_WIDTH + GLA_GATE_RANK + GLA_WIDTH
RWKV_COLS = 3 * RWKV_WIDTH + RWKV_DECAY_RANK + RWKV_A_RANK + RWKV_GATE_RANK
D_IN = MLA_COLS + GLA_COLS + RWKV_COLS
D_FF = 2816
CONV_WIDTH = 3
NORM_EPS = 1e-6

kernel_name = "hybrid_mla_gla_rwkv7_convffn"


def _split(t, sizes):
    out, start = [], 0
    for s in sizes:
        out.append(t[..., start:start + s])
        start += s
    return out


def rms_norm(x, g, eps=NORM_EPS):
    xf = x.astype(jnp.float32)
    y = xf * lax.rsqrt(jnp.mean(xf * xf, axis=-1, keepdims=True) + eps)
    return (y * g.astype(jnp.float32)).astype(x.dtype)


def rope_tables(seq, dim):
    inv = 1.0 / (ROPE_THETA ** (jnp.arange(0, dim, 2, dtype=jnp.float32) / dim))
    ang = jnp.arange(seq, dtype=jnp.float32)[:, None] * inv[None, :]
    return jnp.cos(ang), jnp.sin(ang)


def apply_rope(x, cos, sin):
    half = x.shape[-1] // 2
    xf = x.astype(jnp.float32)
    x1, x2 = xf[..., :half], xf[..., half:]
    return jnp.concatenate([x1 * cos - x2 * sin, x2 * cos + x1 * sin], axis=-1).astype(x.dtype)


def mla_mixer(c_q, c_kv, k_pe, q_norm_g, w_uq, kv_norm_g, w_ukv, out_norm_g, cos, sin):
    B, S, _ = c_q.shape
    q = jnp.einsum('bsr,rf->bsf', rms_norm(c_q, q_norm_g), w_uq).reshape(B, S, MLA_HEADS, MLA_QK)
    q_nope, q_pe = q[..., :MLA_NOPE], q[..., MLA_NOPE:]
    q_pe = apply_rope(q_pe, cos[None, :, None], sin[None, :, None])
    kv = jnp.einsum('bsr,rf->bsf', rms_norm(c_kv, kv_norm_g), w_ukv).reshape(B, S, MLA_HEADS, MLA_NOPE + MLA_V)
    k_nope, v = kv[..., :MLA_NOPE], kv[..., MLA_NOPE:]
    k_pe = apply_rope(k_pe, cos[None], sin[None])
    k = jnp.concatenate([k_nope, jnp.broadcast_to(k_pe[:, :, None, :], (B, S, MLA_HEADS, MLA_ROPE))], axis=-1)
    q = jnp.concatenate([q_nope, q_pe], axis=-1) * (MLA_QK ** -0.5)
    local = jnp.arange(Q_BLOCK)
    outs = []
    for i in range(S // Q_BLOCK):
        end = (i + 1) * Q_BLOCK
        qb = q[:, i * Q_BLOCK:end]
        s = jnp.einsum('bqhd,bkhd->bhqk', qb, k[:, :end], preferred_element_type=jnp.float32)
        mask = (i * Q_BLOCK + local)[:, None] >= jnp.arange(end)[None, :]
        p = jax.nn.softmax(jnp.where(mask, s, -jnp.inf), axis=-1)
        outs.append(jnp.einsum('bhqk,bkhd->bqhd', p.astype(v.dtype), v[:, :end]))
    o = jnp.concatenate(outs, axis=1).reshape(B, S, MLA_WIDTH)
    return rms_norm(o, out_norm_g)


def gla_mixer(q, k, v, gk_lo, g_out, w_gk, b_gk, norm_g):
    B, S, _ = q.shape
    H, DK, DV, C = GLA_HEADS, GLA_DK, GLA_DV, GLA_CHUNK
    N = S // C
    f32 = jnp.float32
    gk = jax.nn.log_sigmoid((jnp.einsum('bsr,rf->bsf', gk_lo, w_gk) + b_gk).astype(f32)) / GLA_GATE_NORM

    def chunks(t, d):
        return t.astype(f32).reshape(B, N, C, H, d).transpose(0, 3, 1, 2, 4)

    qc = chunks(q, DK) * (DK ** -0.5)
    kc, vc, gc = chunks(k, DK), chunks(v, DV), chunks(gk, DK)
    b = jnp.cumsum(gc, axis=3)
    b_last = b[:, :, :, -1:, :]
    q_t = qc * jnp.exp(b)
    k_t = kc * jnp.exp(-b)
    k_end = kc * jnp.exp(b_last - b)
    causal = jnp.tril(jnp.ones((C, C), dtype=bool))
    attn = jnp.where(causal, jnp.einsum('bhnid,bhnjd->bhnij', q_t, k_t), 0.0)
    o_intra = jnp.einsum('bhnij,bhnjv->bhniv', attn, vc)

    def step(state, inp):
        q_n, k_n, v_n, dec_n = inp
        o_n = jnp.einsum('bhcd,bhdv->bhcv', q_n, state)
        state = dec_n[..., None] * state + jnp.einsum('bhcd,bhcv->bhdv', k_n, v_n)
        return state, o_n

    xs = (jnp.moveaxis(q_t, 2, 0), jnp.moveaxis(k_end, 2, 0), jnp.moveaxis(vc, 2, 0),
          jnp.moveaxis(jnp.exp(b_last[:, :, :, 0]), 2, 0))
    _, o_inter = lax.scan(step, jnp.zeros((B, H, DK, DV), f32), xs)
    o = o_intra + jnp.moveaxis(o_inter, 0, 2)
    o = o.transpose(0, 2, 3, 1, 4).reshape(B, S, H, DV)
    o = rms_norm(o, norm_g) * jax.nn.silu(g_out.astype(f32).reshape(B, S, H, DV))
    return o.reshape(B, S, GLA_WIDTH).astype(q.dtype)


def rwkv7_mixer(xc, mu, w0, w2, a0, a2, g2, k_k, k_a, r_k, ln_g, ln_b):
    B, S, _ = xc.shape
    H, N = RWKV_HEADS, RWKV_N
    f32 = jnp.float32
    prev = jnp.pad(xc, ((0, 0), (1, 0), (0, 0)))[:, :-1]
    xm = xc + (prev - xc) * mu
    r, k, v, w_lo, a_lo, g_lo = _split(xm, (RWKV_WIDTH, RWKV_WIDTH, RWKV_WIDTH,
                                            RWKV_DECAY_RANK, RWKV_A_RANK, RWKV_GATE_RANK))
    w = -jax.nn.softplus(-(w0 + jnp.tanh(w_lo) @ w2).astype(f32)) - 0.5
    decay = jnp.exp(-jnp.exp(w))
    a = jax.nn.sigmoid((a0 + a_lo @ a2).astype(f32))
    g = (jax.nn.sigmoid(g_lo) @ g2).astype(f32)

    def heads(t):
        return t.astype(f32).reshape(B, S, H, N)

    kk = heads(k * k_k)
    kk = kk / jnp.maximum(jnp.sqrt(jnp.sum(kk * kk, axis=-1, keepdims=True)), 1e-12)
    k = k.astype(f32) * (1.0 + (a - 1.0) * k_a)
    rh, kh, vh, wh, ah = heads(r), heads(k), heads(v), heads(decay), heads(a)
    a_vec = -kk
    b_vec = kk * ah

    def step(state, inp):
        r_t, w_t, k_t, v_t, a_t, b_t = inp
        sa = jnp.einsum('bhij,bhj->bhi', state, a_t)
        state = (state * w_t[:, :, None, :] + sa[..., None] * b_t[:, :, None, :]
                 + v_t[..., None] * k_t[:, :, None, :])
        return state, jnp.einsum('bhij,bhj->bhi', state, r_t)

    xs = tuple(jnp.moveaxis(t, 1, 0) for t in (rh, wh, kh, vh, a_vec, b_vec))
    _, y = lax.scan(step, jnp.zeros((B, H, N, N), f32), xs)
    y = jnp.moveaxis(y, 0, 1)
    mean = jnp.mean(y, axis=-1, keepdims=True)
    var = jnp.mean(jnp.square(y - mean), axis=-1, keepdims=True)
    y = ((y - mean) * lax.rsqrt(var + RWKV_LN_EPS)).reshape(B, S, RWKV_WIDTH) * ln_g + ln_b
    bonus = jnp.sum(rh * kh * r_k, axis=-1, keepdims=True) * vh
    y = (y + bonus.reshape(B, S, RWKV_WIDTH)) * g
    return y.astype(xc.dtype)


def conv_ffn(h, w_up, conv_w, conv_b, w_down):
    u = jnp.einsum('bsd,df->bsf', h, w_up)
    u = lax.conv_general_dilated(u, conv_w[:, None, :].astype(u.dtype), window_strides=(1,),
                                 padding=[(CONV_WIDTH - 1, 0)],
                                 dimension_numbers=('NWC', 'WIO', 'NWC'),
                                 feature_group_count=u.shape[-1]) + conv_b
    gate, val = u[..., :D_FF], u[..., D_FF:]
    return jnp.einsum('bsf,fd->bsd', jax.nn.silu(gate) * val, w_down)


def setup_inputs(seed: int = 0) -> dict:
    key = jax.random.key(seed)
    ks = iter(jax.random.split(key, 32))
    L = DEPTH

    def nrm(shape, scale):
        return jax.random.normal(next(ks), shape, jnp.float32) * scale

    def gain(shape):
        return 1.0 + 0.02 * jax.random.normal(next(ks), shape, jnp.float32)

    return {
        'x': nrm((BATCH, SEQ, D_MODEL), 1.0),
        'ln1_g': gain((L, D_MODEL)),
        'w_in': nrm((L, D_MODEL, D_IN), D_MODEL ** -0.5),
        'mla_q_norm_g': gain((L, MLA_Q_LORA)),
        'mla_w_uq': nrm((L, MLA_Q_LORA, MLA_HEADS * MLA_QK), MLA_Q_LORA ** -0.5),
        'mla_kv_norm_g': gain((L, MLA_KV_LORA)),
        'mla_w_ukv': nrm((L, MLA_KV_LORA, MLA_HEADS * (MLA_NOPE + MLA_V)), MLA_KV_LORA ** -0.5),
        'mla_out_norm_g': gain((L, MLA_WIDTH)),
        'gla_w_gk': nrm((L, GLA_GATE_RANK, GLA_QK_WIDTH), GLA_GATE_RANK ** -0.5),
        'gla_b_gk': nrm((L, GLA_QK_WIDTH), 0.1),
        'gla_norm_g': gain((L, GLA_DV)),
        'rwkv_mu': jax.random.uniform(next(ks), (L, RWKV_COLS), jnp.float32),
        'rwkv_w0': jax.random.uniform(next(ks), (L, RWKV_WIDTH), jnp.float32, -5.0, 0.0),
        'rwkv_w2': nrm((L, RWKV_DECAY_RANK, RWKV_WIDTH), 0.1 * RWKV_DECAY_RANK ** -0.5),
        'rwkv_a0': nrm((L, RWKV_WIDTH), 0.1),
        'rwkv_a2': nrm((L, RWKV_A_RANK, RWKV_WIDTH), 0.1 * RWKV_A_RANK ** -0.5),
        'rwkv_g2': nrm((L, RWKV_GATE_RANK, RWKV_WIDTH), RWKV_GATE_RANK ** -0.5),
        'rwkv_k_k': 0.85 + nrm((L, RWKV_WIDTH), 0.02),
        'rwkv_k_a': gain((L, RWKV_WIDTH)),
        'rwkv_r_k': nrm((L, RWKV_HEADS, RWKV_N), 0.1),
        'rwkv_ln_g': gain((L, RWKV_WIDTH)),
        'rwkv_ln_b': nrm((L, RWKV_WIDTH), 0.02),
        'w_out': nrm((L, D_MIX, D_MODEL), D_MIX ** -0.5),
        'ln2_g': gain((L, D_MODEL)),
        'ffn_w_up': nrm((L, D_MODEL, 2 * D_FF), D_MODEL ** -0.5),
        'ffn_conv_w': nrm((L, CONV_WIDTH, 2 * D_FF), CONV_WIDTH ** -0.5),
        'ffn_conv_b': nrm((L, 2 * D_FF), 0.02),
        'ffn_w_down': nrm((L, D_FF, D_MODEL), D_FF ** -0.5),
        'final_g': gain((D_MODEL,)),
    }


def reference(x, ln1_g, w_in, mla_q_norm_g, mla_w_uq, mla_kv_norm_g, mla_w_ukv, mla_out_norm_g,
              gla_w_gk, gla_b_gk, gla_norm_g, rwkv_mu, rwkv_w0, rwkv_w2, rwkv_a0, rwkv_a2, rwkv_g2,
              rwkv_k_k, rwkv_k_a, rwkv_r_k, rwkv_ln_g, rwkv_ln_b, w_out, ln2_g, ffn_w_up, ffn_conv_w,
              ffn_conv_b, ffn_w_down, final_g):
    S = x.shape[1]
    cos, sin = rope_tables(S, MLA_ROPE)
    for l in range(DEPTH):
        h = rms_norm(x, ln1_g[l])
        p = jnp.einsum('bsd,dp->bsp', h, w_in[l])
        pa, pb, pc = _split(p, (MLA_COLS, GLA_COLS, RWKV_COLS))
        c_q, c_kv, k_pe = _split(pa, (MLA_Q_LORA, MLA_KV_LORA, MLA_ROPE))
        g_q, g_k, g_v, g_lo, g_out = _split(pb, (GLA_QK_WIDTH, GLA_QK_WIDTH, GLA_WIDTH, GLA_GATE_RANK, GLA_WIDTH))
        y_a = mla_mixer(c_q, c_kv, k_pe, mla_q_norm_g[l], mla_w_uq[l], mla_kv_norm_g[l], mla_w_ukv[l],
                        mla_out_norm_g[l], cos, sin)
        y_b = gla_mixer(g_q, g_k, g_v, g_lo, g_out, gla_w_gk[l], gla_b_gk[l], gla_norm_g[l])
        y_c = rwkv7_mixer(pc, rwkv_mu[l], rwkv_w0[l], rwkv_w2[l], rwkv_a0[l], rwkv_a2[l], rwkv_g2[l],
                          rwkv_k_k[l], rwkv_k_a[l], rwkv_r_k[l], rwkv_ln_g[l], rwkv_ln_b[l])
        y = jnp.concatenate([y_a.astype(x.dtype), y_b.astype(x.dtype), y_c.astype(x.dtype)], axis=-1)
        x = x + jnp.einsum('bsm,md->bsd', y, w_out[l])
        x = x + conv_ffn(rms_norm(x, ln2_g[l]), ffn_w_up[l], ffn_conv_w[l], ffn_conv_b[l], ffn_w_down[l])
    return rms_norm(x, final_g)
```

```python
import functools
import math

import numpy as np
import jax
import jax.numpy as jnp
from jax import lax
from jax.experimental import pallas as pl
from jax.experimental.pallas import tpu as pltpu

F32 = jnp.float32
BF16 = jnp.bfloat16

MLA_HEADS = 4
MLA_NOPE = 128
MLA_ROPE = 64
MLA_V = 128
MLA_QK = MLA_NOPE + MLA_ROPE
MLA_Q_LORA = 384
MLA_KV_LORA = 256
MLA_WIDTH = MLA_HEADS * MLA_V
ROPE_THETA = 10000.0
GLA_HEADS = 4
GLA_DK = 32
GLA_DV = 64
GLA_QK_WIDTH = GLA_HEADS * GLA_DK
GLA_WIDTH = GLA_HEADS * GLA_DV
GLA_GATE_RANK = 16
GLA_GATE_NORM = 16.0
RWKV_HEADS = 4
RWKV_N = 64
RWKV_WIDTH = RWKV_HEADS * RWKV_N
RWKV_DECAY_RANK = 64
RWKV_A_RANK = 64
RWKV_GATE_RANK = 128
RWKV_LN_EPS = 64e-5
MLA_COLS = MLA_Q_LORA + MLA_KV_LORA + MLA_ROPE
GLA_COLS = 2 * GLA_QK_WIDTH + GLA_WIDTH + GLA_GATE_RANK + GLA_WIDTH
RWKV_COLS = 3 * RWKV_WIDTH + RWKV_DECAY_RANK + RWKV_A_RANK + RWKV_GATE_RANK
NORM_EPS = 1e-6
CONV_WIDTH = 3

LANES = 128
SUBLANES = 8
VMEM_LIMIT = 56 * 1024 * 1024

PA_W = MLA_Q_LORA + MLA_KV_LORA + 2 * MLA_ROPE
PB_W = 2 * GLA_QK_WIDTH + 2 * GLA_WIDTH + LANES
PC_W = RWKV_COLS
QK_HEAD_W = 2 * LANES

CHUNK = 64


def _dot(a, b):
    return lax.dot_general(a, b, (((1,), (0,)), ((), ())), preferred_element_type=F32)


def _dot_nt(a, b):
    return lax.dot_general(a, b, (((1,), (1,)), ((), ())), preferred_element_type=F32)


def _dot_tn(a, b):
    return lax.dot_general(a, b, (((0,), (0,)), ((), ())), preferred_element_type=F32)


def _bf(x):
    return x.astype(BF16)


def _split_terms(x, n):
    terms = []
    rem = x
    for _ in range(n):
        t = rem.astype(BF16)
        terms.append(t)
        rem = rem - t.astype(F32)
    return terms


def _exact_left(m, x, n=3):
    out = None
    for t in _split_terms(x, n):
        y = _dot(m, t)
        out = y if out is None else out + y
    return out


def _exact_right(x, m, n=2):
    out = None
    for t in _split_terms(x, n):
        y = _dot(t, m)
        out = y if out is None else out + y
    return out


def _rms(x, g, eps=NORM_EPS):
    return x * lax.rsqrt(jnp.mean(x * x, axis=-1, keepdims=True) + eps) * g


def _sigmoid(x):
    return 1.0 / (1.0 + jnp.exp(-x))


def _softplus(x):
    return jnp.maximum(x, 0.0) + jnp.log(1.0 + jnp.exp(-jnp.abs(x)))


def _params(*sem):
    return pltpu.CompilerParams(dimension_semantics=sem, vmem_limit_bytes=VMEM_LIMIT)


def _in_proj_kernel(x_ref, g_ref, w_ref, pa_ref, pb_ref, pc_ref, *, n_step):
    h = _bf(_rms(x_ref[...], g_ref[...]))
    outs = ((pa_ref, 0, PA_W), (pb_ref, PA_W, PB_W), (pc_ref, PA_W + PB_W, PC_W))
    for ref, base, width in outs:
        for c in range(0, width, n_step):
            w = min(n_step, width - c)
            ref[:, c:c + w] = _dot(h, w_ref[:, base + c:base + c + w])


def _in_proj(x, g, w, *, tm):
    t, d = x.shape
    n = w.shape[1]
    return pl.pallas_call(
        functools.partial(_in_proj_kernel, n_step=512),
        grid=(t // tm,),
        in_specs=[pl.BlockSpec((tm, d), lambda i: (i, 0)),
                  pl.BlockSpec((1, d), lambda i: (0, 0)),
                  pl.BlockSpec((d, n), lambda i: (0, 0))],
        out_specs=[pl.BlockSpec((tm, PA_W), lambda i: (i, 0)),
                   pl.BlockSpec((tm, PB_W), lambda i: (i, 0)),
                   pl.BlockSpec((tm, PC_W), lambda i: (i, 0))],
        out_shape=[jax.ShapeDtypeStruct((t, PA_W), F32),
                   jax.ShapeDtypeStruct((t, PB_W), F32),
                   jax.ShapeDtypeStruct((t, PC_W), F32)],
        compiler_params=_params("parallel"),
        name="in_proj",
    )(x, g, w)


def _mla_prep_kernel(pa_ref, gq_ref, gkv_ref, wq_ref, wkv_ref, rope_ref, q_ref, k_ref, v_ref):
    pa = pa_ref[...]
    rope = rope_ref[...]
    cq = _bf(_rms(pa[:, :MLA_Q_LORA], gq_ref[...]))
    q = _dot(cq, wq_ref[...]) * (MLA_QK ** -0.5)
    for h in range(MLA_HEADS):
        b = h * QK_HEAD_W
        q_ref[:, b:b + LANES] = _bf(q[:, b:b + LANES])
        q_ref[:, b + LANES:b + QK_HEAD_W] = _bf(q[:, b + LANES:b + QK_HEAD_W] * rope)
    ckv = _bf(_rms(pa[:, MLA_Q_LORA:MLA_Q_LORA + MLA_KV_LORA], gkv_ref[...]))
    kv = _dot(ckv, wkv_ref[...])
    kr = pa[:, MLA_Q_LORA + MLA_KV_LORA:] * rope
    kr = _bf(kr + pltpu.roll(kr, MLA_ROPE, 1))
    for h in range(MLA_HEADS):
        b = h * QK_HEAD_W
        k_ref[:, b:b + LANES] = _bf(kv[:, h * MLA_NOPE:(h + 1) * MLA_NOPE])
        k_ref[:, b + LANES:b + QK_HEAD_W] = kr
    v_ref[...] = _bf(kv[:, MLA_HEADS * MLA_NOPE:])


def _mla_prep(pa, gq, gkv, wq, wkv, rope, *, tm, seq):
    t = pa.shape[0]
    n_seq_blocks = seq // tm
    const = lambda i: (0, 0)
    return pl.pallas_call(
        _mla_prep_kernel,
        grid=(t // tm,),
        in_specs=[pl.BlockSpec((tm, PA_W), lambda i: (i, 0)),
                  pl.BlockSpec(gq.shape, const),
                  pl.BlockSpec(gkv.shape, const),
                  pl.BlockSpec(wq.shape, const),
                  pl.BlockSpec(wkv.shape, const),
                  pl.BlockSpec((tm, LANES), lambda i: (i % n_seq_blocks, 0))],
        out_specs=[pl.BlockSpec((tm, MLA_HEADS * QK_HEAD_W), lambda i: (i, 0)),
                   pl.BlockSpec((tm, MLA_HEADS * QK_HEAD_W), lambda i: (i, 0)),
                   pl.BlockSpec((tm, MLA_WIDTH), lambda i: (i, 0))],
        out_shape=[jax.ShapeDtypeStruct((t, MLA_HEADS * QK_HEAD_W), BF16),
                   jax.ShapeDtypeStruct((t, MLA_HEADS * QK_HEAD_W), BF16),
                   jax.ShapeDtypeStruct((t, MLA_WIDTH), BF16)],
        compiler_params=_params("parallel"),
        name="mla_prep",
    )(pa, gq, gkv, wq, wkv, rope)


def _flash_kernel(qi_ref, ki_ref, q_ref, k_ref, v_ref, o_ref, m_sc, l_sc, acc_sc, *, blk):
    p = pl.program_id(2)
    qi = qi_ref[p]
    ki = ki_ref[p]

    @pl.when(ki == 0)
    def _():
        m_sc[...] = jnp.full(m_sc.shape, -jnp.inf, F32)
        l_sc[...] = jnp.zeros(l_sc.shape, F32)
        acc_sc[...] = jnp.zeros(acc_sc.shape, F32)

    def step(masked):
        s = _dot_nt(q_ref[...], k_ref[...])
        if masked:
            row = lax.broadcasted_iota(jnp.int32, s.shape, 0)
            col = lax.broadcasted_iota(jnp.int32, s.shape, 1)
            s = jnp.where(row >= col, s, -jnp.inf)
        m_prev = m_sc[...]
        m_next = jnp.maximum(m_prev, jnp.max(s, axis=1, keepdims=True))
        alpha = jnp.exp(m_prev - m_next)
        pexp = jnp.exp(s - pltpu.repeat(m_next, blk // LANES, 1))
        l_sc[...] = alpha * l_sc[...] + jnp.sum(pexp, axis=1, keepdims=True)
        acc_sc[...] = acc_sc[...] * alpha + _dot(_bf(pexp), v_ref[...])
        m_sc[...] = m_next

    @pl.when(ki < qi)
    def _():
        step(False)

    @pl.when(ki == qi)
    def _():
        step(True)
        o_ref[...] = acc_sc[...] / l_sc[...]


def _flash(q, k, v, *, batch, seq, blk):
    nb = seq // blk
    pairs = [(i, j) for i in range(nb) for j in range(i + 1)]
    qi_t = jnp.asarray(np.array([p[0] for p in pairs], np.int32))
    ki_t = jnp.asarray(np.array([p[1] for p in pairs], np.int32))
    grid_spec = pltpu.PrefetchScalarGridSpec(
        num_scalar_prefetch=2,
        grid=(batch, MLA_HEADS, len(pairs)),
        in_specs=[pl.BlockSpec((blk, QK_HEAD_W), lambda b, h, p, qi, ki: (b * nb + qi[p], h)),
                  pl.BlockSpec((blk, QK_HEAD_W), lambda b, h, p, qi, ki: (b * nb + ki[p], h)),
                  pl.BlockSpec((blk, MLA_V), lambda b, h, p, qi, ki: (b * nb + ki[p], h))],
        out_specs=pl.BlockSpec((blk, MLA_V), lambda b, h, p, qi, ki: (b * nb + qi[p], h)),
        scratch_shapes=[pltpu.VMEM((blk, LANES), F32),
                        pltpu.VMEM((blk, LANES), F32),
                        pltpu.VMEM((blk, MLA_V), F32)])
    return pl.pallas_call(
        functools.partial(_flash_kernel, blk=blk),
        grid_spec=grid_spec,
        out_shape=jax.ShapeDtypeStruct((batch * seq, MLA_WIDTH), F32),
        compiler_params=_params("parallel", "parallel", "arbitrary"),
        name="flash",
    )(qi_t, ki_t, q, k, v)


def _np_masks():
    c = CHUNK
    r4 = np.arange(4 * c)
    tri = (np.arange(c)[:, None] >= np.arange(c)[None, :])
    m = {
        "tri": tri,
        "head64": (r4[:, None] // c) == (np.arange(RWKV_WIDTH)[None, :] // RWKV_N),
        "head32": (r4[:, None] // c) == (np.arange(GLA_QK_WIDTH)[None, :] // GLA_DK),
        "strict": (r4[:, None] % c) > (r4[None, :] % c),
        "causal": np.arange(c)[:, None] >= (r4[None, :] % c),
    }
    return m


def _mask_inputs():
    m = _np_masks()
    return {
        "tri": jnp.asarray(m["tri"], BF16),
        "head64": jnp.asarray(m["head64"], F32),
        "head32": jnp.asarray(m["head32"], F32),
        "strict": jnp.asarray(m["strict"], F32),
        "causal": jnp.asarray(m["causal"], F32),
        "group64": jnp.asarray(m["head64"], BF16),
    }


def _stack(x, head_mask):
    return jnp.concatenate([x] * 4, axis=0) * head_mask


def _unstack(x):
    c = x.shape[0] // 4
    return x[0:c] + x[c:2 * c] + x[2 * c:3 * c] + x[3 * c:4 * c]


def _gla_kernel(qk_ref, v_ref, glo_ref, wgk_ref, bgk_ref, tri_ref, h32_ref, h64_ref, causal_ref,
                o_ref, st_sc, *, n_chunks):
    @pl.when(pl.program_id(1) == 0)
    def _():
        st_sc[...] = jnp.zeros(st_sc.shape, F32)

    c = CHUNK
    tri = tri_ref[...]
    h32 = h32_ref[...]
    h64 = h64_ref[...]
    causal = causal_ref[...]
    for j in range(n_chunks):
        rows = slice(j * c, (j + 1) * c)
        q = qk_ref[rows, :GLA_QK_WIDTH]
        k = qk_ref[rows, GLA_QK_WIDTH:]
        v = v_ref[rows, :]
        pre = _dot(_bf(glo_ref[rows, :]), wgk_ref[...]) + bgk_ref[...]
        gk = -_softplus(-pre) / GLA_GATE_NORM
        b = _exact_left(tri, gk)
        b_last = b[c - 1:c, :]
        q_t = q * (GLA_DK ** -0.5) * jnp.exp(b)
        k_t = k * jnp.exp(-b)
        k_end = k * jnp.exp(b_last - b)
        a = _dot_nt(_bf(q_t), _bf(_stack(k_t, h32)))
        a = jnp.where(causal != 0, a, 0.0)
        st = st_sc[...]
        o = _dot(_bf(a), _bf(_stack(v, h64))) + _dot_nt(_bf(q_t), _bf(st))
        o_ref[rows, :] = o
        upd = _dot_tn(_bf(v), _bf(k_end))
        st_sc[...] = st * jnp.exp(b_last) + upd * h32


def _gla(pb, wgk, bgk, masks, *, batch, seq, tc):
    t = pb.shape[0]
    nsb = seq // tc
    const = lambda b, s: (0, 0)
    row = lambda b, s: b * nsb + s
    return pl.pallas_call(
        functools.partial(_gla_kernel, n_chunks=tc // CHUNK),
        grid=(batch, nsb),
        in_specs=[pl.BlockSpec((tc, 2 * GLA_QK_WIDTH), lambda b, s: (row(b, s), 0)),
                  pl.BlockSpec((tc, GLA_WIDTH), lambda b, s: (row(b, s), 1)),
                  pl.BlockSpec((tc, LANES), lambda b, s: (row(b, s), 6)),
                  pl.BlockSpec(wgk.shape, const),
                  pl.BlockSpec(bgk.shape, const),
                  pl.BlockSpec(masks["tri"].shape, const),
                  pl.BlockSpec(masks["head32"].shape, const),
                  pl.BlockSpec(masks["head64"].shape, const),
                  pl.BlockSpec(masks["causal"].shape, const)],
        out_specs=pl.BlockSpec((tc, GLA_WIDTH), lambda b, s: (row(b, s), 0)),
        out_shape=jax.ShapeDtypeStruct((t, GLA_WIDTH), F32),
        scratch_shapes=[pltpu.VMEM((GLA_WIDTH, GLA_QK_WIDTH), F32)],
        compiler_params=_params("parallel", "arbitrary"),
        name="gla",
    )(pb, pb, pb, wgk, bgk, masks["tri"], masks["head32"], masks["head64"], masks["causal"])


def _rwkv_prep_kernel(pc_ref, halo_ref, mu_ref, w0_ref, w2_ref, a0_ref, a2_ref, g2_ref, kk_ref, ka_ref,
                      rk_ref, grp_ref, r_out, lw_out, k_out, v_out, kk_out, b_out, g_out, bonus_out,
                      *, blocks_per_seq):
    xc = pc_ref[...]
    tm = xc.shape[0]
    first = (pl.program_id(0) % blocks_per_seq) == 0
    last_prev = jnp.where(first, 0.0, halo_ref[SUBLANES - 1:SUBLANES, :])
    row = lax.broadcasted_iota(jnp.int32, xc.shape, 0)
    prev = jnp.where(row == 0, last_prev, pltpu.roll(xc, 1, 0))
    xm = xc + (prev - xc) * mu_ref[...]
    w3 = RWKV_WIDTH
    r = xm[:, 0:w3]
    k = xm[:, w3:2 * w3]
    v = xm[:, 2 * w3:3 * w3]
    wa = xm[:, 3 * w3:3 * w3 + LANES]
    g_lo = xm[:, 3 * w3 + LANES:]
    w = -_softplus(-(w0_ref[...] + _dot(_bf(jnp.tanh(wa)), w2_ref[...]))) - 0.5
    a = _sigmoid(a0_ref[...] + _dot(_bf(wa), a2_ref[...]))
    g = _dot(_bf(_sigmoid(g_lo)), g2_ref[...])
    grp = grp_ref[...]
    kk = k * kk_ref[...]
    norm = jnp.sqrt(_exact_right(kk * kk, grp))
    kk = kk / jnp.maximum(norm, 1e-12)
    k2 = k * (1.0 + (a - 1.0) * ka_ref[...])
    r_out[...] = r
    lw_out[...] = -jnp.exp(w)
    k_out[...] = k2
    v_out[...] = v
    kk_out[...] = kk
    b_out[...] = kk * a
    g_out[...] = g
    bonus_out[...] = _exact_right(r * k2 * rk_ref[...], grp) * v


def _rwkv_prep(pc, mu, w0, w2p, a0, a2p, g2, k_k, k_a, r_k, grp, *, tm, seq):
    t = pc.shape[0]
    const = lambda i: (0, 0)
    hb = tm // SUBLANES
    out = jax.ShapeDtypeStruct((t, RWKV_WIDTH), F32)
    ospec = pl.BlockSpec((tm, RWKV_WIDTH), lambda i: (i, 0))
    small = [mu, w0, w2p, a0, a2p, g2, k_k, k_a, r_k, grp]
    return pl.pallas_call(
        functools.partial(_rwkv_prep_kernel, blocks_per_seq=seq // tm),
        grid=(t // tm,),
        in_specs=[pl.BlockSpec((tm, PC_W), lambda i: (i, 0)),
                  pl.BlockSpec((SUBLANES, PC_W), lambda i: (jnp.maximum(i * hb - 1, 0), 0))]
                 + [pl.BlockSpec(a.shape, const) for a in small],
        out_specs=[ospec] * 8,
        out_shape=[out] * 8,
        compiler_params=_params("parallel"),
        name="rwkv_prep",
    )(pc, pc, *small)


def _rwkv_kernel(r_ref, lw_ref, k_ref, v_ref, kk_ref, b_ref, tri_ref, h64_ref, strict_ref, causal_ref,
                 y_ref, s_sc):
    @pl.when(pl.program_id(1) == 0)
    def _():
        s_sc[...] = jnp.zeros(s_sc.shape, F32)

    c = CHUNK
    h64 = h64_ref[...]
    lw = lw_ref[...]
    g = _exact_left(tri_ref[...], lw)
    g_last = g[c - 1:c, :]
    e_neg = jnp.exp(-g)
    a_t = -kk_ref[...] * jnp.exp(g - lw)
    r_t = r_ref[...] * jnp.exp(g)
    b_t = b_ref[...] * e_neg
    k_t = k_ref[...] * e_neg
    e_end = jnp.exp(g_last - g)
    b_end = b_ref[...] * e_end
    k_end = k_ref[...] * e_end
    v = v_ref[...]

    a_s = _bf(_stack(a_t, h64))
    bk_s = _bf(jnp.concatenate([_stack(b_t, h64), _stack(k_t, h64)], axis=0))
    v_s = _bf(_stack(v, h64))
    strict = strict_ref[...] != 0
    causal = causal_ref[...] != 0
    ab_ak = _dot_nt(a_s, bk_s)
    l1 = jnp.where(strict, ab_ak[:, :4 * c], 0.0)
    a_ak = jnp.where(strict, ab_ak[:, 4 * c:], 0.0)
    rb_rk = _dot_nt(_bf(r_t), bk_s)
    a_rb = jnp.where(causal, rb_rk[:, :4 * c], 0.0)
    a_rk = jnp.where(causal, rb_rk[:, 4 * c:], 0.0)

    eye = (lax.broadcasted_iota(jnp.int32, l1.shape, 0) == lax.broadcasted_iota(jnp.int32, l1.shape, 1)).astype(F32)
    t_inv = eye + l1
    lp = l1
    for _ in range(5):
        lpb = _bf(lp)
        lp = _dot(lpb, lpb)
        t_inv = t_inv + _dot(_bf(t_inv), _bf(lp))
    t_b = _bf(t_inv)
    w_a = _unstack(_dot(t_b, a_s))
    u_v = _unstack(_dot(t_b, _bf(_dot(_bf(a_ak), v_s))))

    s = s_sc[...]
    s_b = _bf(s)
    u = _dot_nt(_bf(w_a), s_b) + u_v
    y = _dot_nt(_bf(r_t), s_b) + _dot(_bf(a_rb), _bf(_stack(u, h64))) + _dot(_bf(a_rk), v_s)
    y_ref[...] = y
    uv = _bf(jnp.concatenate([u, v], axis=0))
    bk_end = _bf(jnp.concatenate([b_end, k_end], axis=0))
    s_sc[...] = s * jnp.exp(g_last) + _dot_tn(uv, bk_end) * h64


def _rwkv(r, lw, k2, v, kk, bvec, masks, *, batch, seq):
    t = r.shape[0]
    nsb = seq // CHUNK
    const = lambda b, s: (0, 0)
    spec = pl.BlockSpec((CHUNK, RWKV_WIDTH), lambda b, s: (b * nsb + s, 0))
    return pl.pallas_call(
        _rwkv_kernel,
        grid=(batch, nsb),
        in_specs=[spec] * 6 + [pl.BlockSpec(masks["tri"].shape, const),
                               pl.BlockSpec(masks["head64"].shape, const),
                               pl.BlockSpec(masks["strict"].shape, const),
                               pl.BlockSpec(masks["causal"].shape, const)],
        out_specs=spec,
        out_shape=jax.ShapeDtypeStruct((t, RWKV_WIDTH), F32),
        scratch_shapes=[pltpu.VMEM((RWKV_WIDTH, RWKV_WIDTH), F32)],
        compiler_params=_params("parallel", "arbitrary"),
        name="rwkv",
    )(r, lw, k2, v, kk, bvec, masks["tri"], masks["head64"], masks["strict"], masks["causal"])


def _mix_out_kernel(x_ref, oa_ref, ob_ref, gout_ref, yc_ref, bonus_ref, gate_ref, na_ref, nb_ref,
                    lng_ref, lnb_ref, grp_ref, wa_ref, wb_ref, wc_ref, o_ref):
    grp = grp_ref[...]
    ya = _rms(oa_ref[...], na_ref[...])
    ob = ob_ref[...]
    ms = _exact_right(ob * ob, grp) * (1.0 / GLA_DV)
    gout = gout_ref[...]
    yb = ob * lax.rsqrt(ms + NORM_EPS) * nb_ref[...] * (gout * _sigmoid(gout))
    y = yc_ref[...]
    mean = _exact_right(y, grp) * (1.0 / RWKV_N)
    d = y - mean
    var = _exact_right(d * d, grp) * (1.0 / RWKV_N)
    yc = (d * lax.rsqrt(var + RWKV_LN_EPS) * lng_ref[...] + lnb_ref[...] + bonus_ref[...]) * gate_ref[...]
    o_ref[...] = (x_ref[...] + _dot(_bf(ya), wa_ref[...]) + _dot(_bf(yb), wb_ref[...])
                  + _dot(_bf(yc), wc_ref[...]))


def _mix_out(x, oa, ob, pb, yc, bonus, gate, na, nb, lng, lnb, grp, wa, wb, wc, *, tm):
    t, d = x.shape
    const = lambda i: (0, 0)
    row = lambda w: pl.BlockSpec((tm, w), lambda i: (i, 0))
    small = [na, nb, lng, lnb, grp, wa, wb, wc]
    return pl.pallas_call(
        _mix_out_kernel,
        grid=(t // tm,),
        in_specs=[row(d), row(MLA_WIDTH), row(GLA_WIDTH),
                  pl.BlockSpec((tm, GLA_WIDTH), lambda i: (i, 2)),
                  row(RWKV_WIDTH), row(RWKV_WIDTH), row(RWKV_WIDTH)]
                 + [pl.BlockSpec(a.shape, const) for a in small],
        out_specs=row(d),
        out_shape=jax.ShapeDtypeStruct((t, d), F32),
        compiler_params=_params("parallel"),
        name="mix_out",
    )(x, oa, ob, pb, yc, bonus, gate, *small)


def _causal_conv3(u, prev, cw, cb):
    row = lax.broadcasted_iota(jnp.int32, u.shape, 0)
    u1 = jnp.where(row == 0, prev[SUBLANES - 1:SUBLANES, :], pltpu.roll(u, 1, 0))
    u2 = jnp.where(row == 0, prev[SUBLANES - 2:SUBLANES - 1, :],
                   jnp.where(row == 1, prev[SUBLANES - 1:SUBLANES, :], pltpu.roll(u, 2, 0)))
    return cw[0:1, :] * u2 + cw[1:2, :] * u1 + cw[2:3, :] * u + cb


def _ffn_kernel(x_ref, g_ref, wg_ref, wv_ref, cwg_ref, cwv_ref, cbg_ref, cbv_ref, wd_ref, fg_ref,
                o_ref, h_sc, acc_sc, cg_sc, cv_sc, *, n_f, final_norm):
    s = pl.program_id(1)
    f = pl.program_id(2)

    @pl.when(f == 0)
    def _():
        h_sc[...] = _bf(_rms(x_ref[...], g_ref[...]))
        acc_sc[...] = jnp.zeros(acc_sc.shape, F32)

    h = h_sc[...]
    ts = h.shape[0]
    ug = _dot(h, wg_ref[...])
    uv = _dot(h, wv_ref[...])
    first = s == 0
    gate = _causal_conv3(ug, jnp.where(first, 0.0, cg_sc[f]), cwg_ref[...], cbg_ref[...])
    val = _causal_conv3(uv, jnp.where(first, 0.0, cv_sc[f]), cwv_ref[...], cbv_ref[...])
    cg_sc[f] = ug[ts - SUBLANES:, :]
    cv_sc[f] = uv[ts - SUBLANES:, :]
    act = gate * _sigmoid(gate) * val
    acc_sc[...] += _dot(_bf(act), wd_ref[...])

    @pl.when(f == n_f - 1)
    def _():
        y = x_ref[...] + acc_sc[...]
        if final_norm:
            y = _rms(y, fg_ref[...])
        o_ref[...] = y


def _ffn(x, g, w_up, conv_w, conv_b, w_down, final_g, *, batch, seq, ts, tf, final_norm):
    t, d = x.shape
    d_ff = w_down.shape[0]
    n_f = d_ff // tf
    nsb = seq // ts
    row = lambda b, s, f: (b * nsb + s, 0)
    const = lambda b, s, f: (0, 0)
    return pl.pallas_call(
        functools.partial(_ffn_kernel, n_f=n_f, final_norm=final_norm),
        grid=(batch, nsb, n_f),
        in_specs=[pl.BlockSpec((ts, d), row),
                  pl.BlockSpec((1, d), const),
                  pl.BlockSpec((d, tf), lambda b, s, f: (0, f)),
                  pl.BlockSpec((d, tf), lambda b, s, f: (0, n_f + f)),
                  pl.BlockSpec((CONV_WIDTH, tf), lambda b, s, f: (0, f)),
                  pl.BlockSpec((CONV_WIDTH, tf), lambda b, s, f: (0, n_f + f)),
                  pl.BlockSpec((1, tf), lambda b, s, f: (0, f)),
                  pl.BlockSpec((1, tf), lambda b, s, f: (0, n_f + f)),
                  pl.BlockSpec((tf, d), lambda b, s, f: (f, 0)),
                  pl.BlockSpec((1, d), const)],
        out_specs=pl.BlockSpec((ts, d), row),
        out_shape=jax.ShapeDtypeStruct((t, d), F32),
        scratch_shapes=[pltpu.VMEM((ts, d), BF16),
                        pltpu.VMEM((ts, d), F32),
                        pltpu.VMEM((n_f, SUBLANES, tf), F32),
                        pltpu.VMEM((n_f, SUBLANES, tf), F32)],
        compiler_params=_params("parallel", "arbitrary", "arbitrary"),
        name="ffn",
    )(x, g, w_up, w_up, conv_w, conv_w, conv_b, conv_b, w_down, final_g)


def _rot_cols(w):
    half = w.shape[-1] // 2
    return jnp.concatenate([-w[..., half:], w[..., :half]], axis=-1)


def _regroup_w_in(w_in):
    d = w_in.shape[0]
    a0 = 0
    b0 = MLA_COLS
    c0 = MLA_COLS + GLA_COLS
    k_pe = w_in[:, MLA_Q_LORA + MLA_KV_LORA:MLA_COLS]
    gq = w_in[:, b0:b0 + GLA_QK_WIDTH]
    gk = w_in[:, b0 + GLA_QK_WIDTH:b0 + 2 * GLA_QK_WIDTH]
    gv = w_in[:, b0 + 2 * GLA_QK_WIDTH:b0 + 2 * GLA_QK_WIDTH + GLA_WIDTH]
    glo = w_in[:, b0 + 2 * GLA_QK_WIDTH + GLA_WIDTH:b0 + 2 * GLA_QK_WIDTH + GLA_WIDTH + GLA_GATE_RANK]
    gout = w_in[:, b0 + 2 * GLA_QK_WIDTH + GLA_WIDTH + GLA_GATE_RANK:c0]
    pad = jnp.zeros((d, LANES - GLA_GATE_RANK), w_in.dtype)
    cols = [w_in[:, a0:MLA_COLS], _rot_cols(k_pe), gq, gk, gv, gout, glo, pad, w_in[:, c0:]]
    return _bf(jnp.concatenate(cols, axis=1))


def _regroup_w_uq(w_uq):
    r = w_uq.shape[0]
    w = w_uq.reshape(r, MLA_HEADS, MLA_QK)
    nope = w[:, :, :MLA_NOPE]
    pe = w[:, :, MLA_NOPE:]
    return _bf(jnp.concatenate([nope, pe, _rot_cols(pe)], axis=-1).reshape(r, MLA_HEADS * QK_HEAD_W))


def _regroup_w_ukv(w_ukv):
    r = w_ukv.shape[0]
    w = w_ukv.reshape(r, MLA_HEADS, MLA_NOPE + MLA_V)
    k_nope = w[:, :, :MLA_NOPE].reshape(r, MLA_HEADS * MLA_NOPE)
    v = w[:, :, MLA_NOPE:].reshape(r, MLA_HEADS * MLA_V)
    return _bf(jnp.concatenate([k_nope, v], axis=1))


def _rope_table(seq):
    inv = 1.0 / (ROPE_THETA ** (jnp.arange(0, MLA_ROPE, 2, dtype=F32) / MLA_ROPE))
    ang = jnp.arange(seq, dtype=F32)[:, None] * inv[None, :]
    cos, sin = jnp.cos(ang), jnp.sin(ang)
    return jnp.concatenate([cos, cos, sin, sin], axis=1)


def _row(v):
    return v.reshape(1, -1).astype(F32)


def _pad_rows(w, top, total):
    return jnp.concatenate([jnp.zeros((top, w.shape[1]), w.dtype), w,
                            jnp.zeros((total - top - w.shape[0], w.shape[1]), w.dtype)], axis=0)


def _tiles(seq):
    return {
        "tm": min(512, seq),
        "blk": min(512, seq),
        "tc": min(256, seq),
        "ts": min(512, seq),
        "tf": 256,
    }


def kernel(x, ln1_g, w_in, mla_q_norm_g, mla_w_uq, mla_kv_norm_g, mla_w_ukv, mla_out_norm_g, gla_w_gk, gla_b_gk, gla_norm_g, rwkv_mu, rwkv_w0, rwkv_w2, rwkv_a0, rwkv_a2, rwkv_g2, rwkv_k_k, rwkv_k_a, rwkv_r_k, rwkv_ln_g, rwkv_ln_b, w_out, ln2_g, ffn_w_up, ffn_conv_w, ffn_conv_b, ffn_w_down, final_g):
    batch, seq, d_model = x.shape
    depth = w_in.shape[0]
    tl = _tiles(seq)
    masks = _mask_inputs()
    grp = masks["group64"]
    rope = _rope_table(seq)
    xt = x.reshape(batch * seq, d_model)
    for l in range(depth):
        pa, pb, pc = _in_proj(xt, _row(ln1_g[l]), _regroup_w_in(w_in[l]), tm=tl["tm"])
        q, k, v = _mla_prep(pa, _row(mla_q_norm_g[l]), _row(mla_kv_norm_g[l]), _regroup_w_uq(mla_w_uq[l]),
                            _regroup_w_ukv(mla_w_ukv[l]), rope, tm=tl["tm"], seq=seq)
        o_mla = _flash(q, k, v, batch=batch, seq=seq, blk=tl["blk"])
        wgk = _bf(_pad_rows(gla_w_gk[l], 0, LANES))
        o_gla = _gla(pb, wgk, _row(gla_b_gk[l]), masks, batch=batch, seq=seq, tc=tl["tc"])
        w2p = _bf(_pad_rows(rwkv_w2[l], 0, LANES))
        a2p = _bf(_pad_rows(rwkv_a2[l], RWKV_DECAY_RANK, LANES))
        r, lw, k2, vv, kk, bvec, gate, bonus = _rwkv_prep(
            pc, _row(rwkv_mu[l]), _row(rwkv_w0[l]), w2p, _row(rwkv_a0[l]), a2p, _bf(rwkv_g2[l]),
            _row(rwkv_k_k[l]), _row(rwkv_k_a[l]), _row(rwkv_r_k[l]), grp, tm=tl["tm"], seq=seq)
        y_rwkv = _rwkv(r, lw, k2, vv, kk, bvec, masks, batch=batch, seq=seq)
        wo = _bf(w_out[l])
        xt = _mix_out(xt, o_mla, o_gla, pb, y_rwkv, bonus, gate, _row(mla_out_norm_g[l]),
                      _row(jnp.tile(gla_norm_g[l], GLA_HEADS)), _row(rwkv_ln_g[l]), _row(rwkv_ln_b[l]), grp,
                      wo[:MLA_WIDTH], wo[MLA_WIDTH:MLA_WIDTH + GLA_WIDTH], wo[MLA_WIDTH + GLA_WIDTH:],
                      tm=tl["tm"])
        xt = _ffn(xt, _row(ln2_g[l]), _bf(ffn_w_up[l]), ffn_conv_w[l].astype(F32), _row(ffn_conv_b[l]),
                  _bf(ffn_w_down[l]), _row(final_g), batch=batch, seq=seq, ts=tl["ts"], tf=tl["tf"],
                  final_norm=(l == depth - 1))
    return xt.reshape(batch, seq, d_model)
```

```python
import functools
import math

import numpy as np
import jax
import jax.numpy as jnp
from jax import lax
from jax.experimental import pallas as pl
from jax.experimental.pallas import tpu as pltpu

F32 = jnp.float32
BF16 = jnp.bfloat16

MLA_HEADS = 4
MLA_NOPE = 128
MLA_ROPE = 64
MLA_V = 128
MLA_QK = MLA_NOPE + MLA_ROPE
MLA_Q_LORA = 384
MLA_KV_LORA = 256
MLA_WIDTH = MLA_HEADS * MLA_V
ROPE_THETA = 10000.0
GLA_HEADS = 4
GLA_DK = 32
GLA_DV = 64
GLA_QK_WIDTH = GLA_HEADS * GLA_DK
GLA_WIDTH = GLA_HEADS * GLA_DV
GLA_GATE_RANK = 16
GLA_GATE_NORM = 16.0
RWKV_HEADS = 4
RWKV_N = 64
RWKV_WIDTH = RWKV_HEADS * RWKV_N
RWKV_DECAY_RANK = 64
RWKV_A_RANK = 64
RWKV_GATE_RANK = 128
RWKV_LN_EPS = 64e-5
MLA_COLS = MLA_Q_LORA + MLA_KV_LORA + MLA_ROPE
GLA_COLS = 2 * GLA_QK_WIDTH + GLA_WIDTH + GLA_GATE_RANK + GLA_WIDTH
RWKV_COLS = 3 * RWKV_WIDTH + RWKV_DECAY_RANK + RWKV_A_RANK + RWKV_GATE_RANK
NORM_EPS = 1e-6
CONV_WIDTH = 3

LANES = 128
SUBLANES = 8
VMEM_LIMIT = 56 * 1024 * 1024

PA_W = MLA_Q_LORA + MLA_KV_LORA + 2 * MLA_ROPE
PB_W = 2 * GLA_QK_WIDTH + 2 * GLA_WIDTH + LANES
PC_W = RWKV_COLS
QK_HEAD_W = 2 * LANES

LOG2E = math.log2(math.e)

CHUNK = 64


def _dot(a, b):
    return lax.dot_general(a, b, (((1,), (0,)), ((), ())), preferred_element_type=F32)


def _dot_nt(a, b):
    return lax.dot_general(a, b, (((1,), (1,)), ((), ())), preferred_element_type=F32)


def _dot_tn(a, b):
    return lax.dot_general(a, b, (((0,), (0,)), ((), ())), preferred_element_type=F32)


def _bf(x):
    return x.astype(BF16)


def _split_terms(x, n):
    terms = []
    rem = x
    for _ in range(n):
        t = rem.astype(BF16)
        terms.append(t)
        rem = rem - t.astype(F32)
    return terms


def _exact_left(m, x, n=3):
    out = None
    for t in _split_terms(x, n):
        y = _dot(m, t)
        out = y if out is None else out + y
    return out


def _exact_right(x, m, n=2):
    out = None
    for t in _split_terms(x, n):
        y = _dot(t, m)
        out = y if out is None else out + y
    return out


def _rms(x, g, eps=NORM_EPS):
    return x * lax.rsqrt(jnp.mean(x * x, axis=-1, keepdims=True) + eps) * g


def _sigmoid(x):
    return 1.0 / (1.0 + jnp.exp(-x))


def _softplus(x):
    return jnp.maximum(x, 0.0) + jnp.log(1.0 + jnp.exp(-jnp.abs(x)))


def _params(*sem):
    return pltpu.CompilerParams(dimension_semantics=sem, vmem_limit_bytes=VMEM_LIMIT)


def _in_proj_kernel(x_ref, g_ref, w_ref, pa_ref, pb_ref, pc_ref, *, n_step):
    h = _bf(_rms(x_ref[...], g_ref[...]))
    outs = ((pa_ref, 0, PA_W), (pb_ref, PA_W, PB_W), (pc_ref, PA_W + PB_W, PC_W))
    for ref, base, width in outs:
        for c in range(0, width, n_step):
            w = min(n_step, width - c)
            ref[:, c:c + w] = _dot(h, w_ref[:, base + c:base + c + w])


def _in_proj(x, g, w, *, tm):
    t, d = x.shape
    n = w.shape[1]
    return pl.pallas_call(
        functools.partial(_in_proj_kernel, n_step=512),
        grid=(t // tm,),
        in_specs=[pl.BlockSpec((tm, d), lambda i: (i, 0)),
                  pl.BlockSpec((1, d), lambda i: (0, 0)),
                  pl.BlockSpec((d, n), lambda i: (0, 0))],
        out_specs=[pl.BlockSpec((tm, PA_W), lambda i: (i, 0)),
                   pl.BlockSpec((tm, PB_W), lambda i: (i, 0)),
                   pl.BlockSpec((tm, PC_W), lambda i: (i, 0))],
        out_shape=[jax.ShapeDtypeStruct((t, PA_W), F32),
                   jax.ShapeDtypeStruct((t, PB_W), F32),
                   jax.ShapeDtypeStruct((t, PC_W), F32)],
        compiler_params=_params("parallel"),
        name="in_proj",
    )(x, g, w)


def _mla_prep_kernel(pa_ref, gq_ref, gkv_ref, wq_ref, wkv_ref, rope_ref, q_ref, k_ref, v_ref):
    pa = pa_ref[...]
    rope = rope_ref[...]
    cq = _bf(_rms(pa[:, :MLA_Q_LORA], gq_ref[...]))
    q = _dot(cq, wq_ref[...]) * (MLA_QK ** -0.5 * LOG2E)
    for h in range(MLA_HEADS):
        b = h * QK_HEAD_W
        q_ref[:, b:b + LANES] = _bf(q[:, b:b + LANES])
        q_ref[:, b + LANES:b + QK_HEAD_W] = _bf(q[:, b + LANES:b + QK_HEAD_W] * rope)
    ckv = _bf(_rms(pa[:, MLA_Q_LORA:MLA_Q_LORA + MLA_KV_LORA], gkv_ref[...]))
    kv = _dot(ckv, wkv_ref[...])
    kr = pa[:, MLA_Q_LORA + MLA_KV_LORA:] * rope
    kr = _bf(kr + pltpu.roll(kr, MLA_ROPE, 1))
    for h in range(MLA_HEADS):
        b = h * QK_HEAD_W
        k_ref[:, b:b + LANES] = _bf(kv[:, h * MLA_NOPE:(h + 1) * MLA_NOPE])
        k_ref[:, b + LANES:b + QK_HEAD_W] = kr
    ones = jnp.ones((pa.shape[0], MLA_V), BF16)
    for h in range(MLA_HEADS):
        b = h * 2 * MLA_V
        v_ref[:, b:b + MLA_V] = _bf(kv[:, (MLA_HEADS + h) * MLA_NOPE:(MLA_HEADS + h + 1) * MLA_NOPE])
        v_ref[:, b + MLA_V:b + 2 * MLA_V] = ones


def _mla_prep(pa, gq, gkv, wq, wkv, rope, *, tm, seq):
    t = pa.shape[0]
    n_seq_blocks = seq // tm
    const = lambda i: (0, 0)
    return pl.pallas_call(
        _mla_prep_kernel,
        grid=(t // tm,),
        in_specs=[pl.BlockSpec((tm, PA_W), lambda i: (i, 0)),
                  pl.BlockSpec(gq.shape, const),
                  pl.BlockSpec(gkv.shape, const),
                  pl.BlockSpec(wq.shape, const),
                  pl.BlockSpec(wkv.shape, const),
                  pl.BlockSpec((tm, LANES), lambda i: (i % n_seq_blocks, 0))],
        out_specs=[pl.BlockSpec((tm, MLA_HEADS * QK_HEAD_W), lambda i: (i, 0)),
                   pl.BlockSpec((tm, MLA_HEADS * QK_HEAD_W), lambda i: (i, 0)),
                   pl.BlockSpec((tm, 2 * MLA_WIDTH), lambda i: (i, 0))],
        out_shape=[jax.ShapeDtypeStruct((t, MLA_HEADS * QK_HEAD_W), BF16),
                   jax.ShapeDtypeStruct((t, MLA_HEADS * QK_HEAD_W), BF16),
                   jax.ShapeDtypeStruct((t, 2 * MLA_WIDTH), BF16)],
        compiler_params=_params("parallel"),
        name="mla_prep",
    )(pa, gq, gkv, wq, wkv, rope)


def _flash_kernel(q_ref, k_ref, v_ref, o_ref, m_sc, acc_sc, s_sc, *, tk):
    qi = pl.program_id(2)
    m_sc[...] = jnp.full(m_sc.shape, -jnp.inf, F32)
    acc_sc[...] = jnp.zeros(acc_sc.shape, F32)

    def kv_rows(j):
        return pl.ds(pl.multiple_of(j * tk, tk), tk)

    def scores(sub, j):
        return _dot_nt(q_ref[sub * tk:(sub + 1) * tk, :], k_ref[kv_rows(j), :])

    def accumulate(sub, j, s, masked):
        rows = slice(sub * tk, (sub + 1) * tk)
        if masked:
            row = lax.broadcasted_iota(jnp.int32, s.shape, 0)
            col = lax.broadcasted_iota(jnp.int32, s.shape, 1)
            s = jnp.where(row >= col, s, -jnp.inf)
        m_prev = m_sc[rows, :]
        m_next = jnp.maximum(m_prev, jnp.max(s, axis=1, keepdims=True))
        alpha = jnp.exp2(m_prev - m_next)
        p = jnp.exp2(s - pltpu.repeat(m_next, tk // LANES, 1))
        acc_sc[rows, :] = acc_sc[rows, :] * pltpu.repeat(alpha, 2, 1) + _dot(_bf(p), v_ref[kv_rows(j), :])
        m_sc[rows, :] = m_next

    for sub in range(2):
        s_sc[0, sub] = scores(sub, 0)

    def body(jj, carry):
        j = 2 * jj
        for slot in range(2):
            for sub in range(2):
                s = s_sc[slot, sub]
                s_sc[1 - slot, sub] = scores(sub, j + slot + 1)
                accumulate(sub, j + slot, s, False)
        return carry

    lax.fori_loop(0, qi, body, 0)
    j = 2 * qi
    s_last = scores(1, j + 1)
    accumulate(0, j, s_sc[0, 0], True)
    accumulate(1, j, s_sc[0, 1], False)
    accumulate(1, j + 1, s_last, True)
    acc = acc_sc[...]
    o_ref[...] = acc[:, :MLA_V] / acc[:, MLA_V:]


def _flash(q, k, v1, *, batch, seq, tk):
    tq = 2 * tk
    nq = seq // tq
    return pl.pallas_call(
        functools.partial(_flash_kernel, tk=tk),
        grid=(batch, MLA_HEADS, nq),
        in_specs=[pl.BlockSpec((tq, QK_HEAD_W), lambda b, h, i: (b * nq + i, h)),
                  pl.BlockSpec((seq, QK_HEAD_W), lambda b, h, i: (b, h)),
                  pl.BlockSpec((seq, 2 * MLA_V), lambda b, h, i: (b, h))],
        out_specs=pl.BlockSpec((tq, MLA_V), lambda b, h, i: (b * nq + i, h)),
        out_shape=jax.ShapeDtypeStruct((batch * seq, MLA_WIDTH), F32),
        scratch_shapes=[pltpu.VMEM((tq, LANES), F32),
                        pltpu.VMEM((tq, 2 * MLA_V), F32),
                        pltpu.VMEM((2, 2, tk, tk), F32)],
        compiler_params=_params("parallel", "parallel", "arbitrary"),
        name="flash",
    )(q, k, v1)


def _np_masks():
    c = CHUNK
    r4 = np.arange(4 * c)
    tri = (np.arange(c)[:, None] >= np.arange(c)[None, :])
    m = {
        "tri": tri,
        "head64": (r4[:, None] // c) == (np.arange(RWKV_WIDTH)[None, :] // RWKV_N),
        "head32": (r4[:, None] // c) == (np.arange(GLA_QK_WIDTH)[None, :] // GLA_DK),
        "strict": (r4[:, None] % c) > (r4[None, :] % c),
        "causal": np.arange(c)[:, None] >= (r4[None, :] % c),
    }
    return m


def _mask_inputs():
    m = _np_masks()
    return {
        "tri": jnp.asarray(m["tri"], BF16),
        "head64": jnp.asarray(m["head64"], F32),
        "head32": jnp.asarray(m["head32"], F32),
        "strict": jnp.asarray(m["strict"], F32),
        "causal": jnp.asarray(m["causal"], F32),
        "group64": jnp.asarray(m["head64"], BF16),
    }


def _stack(x, head_mask):
    return jnp.concatenate([x] * 4, axis=0) * head_mask


def _unstack(x):
    c = x.shape[0] // 4
    return x[0:c] + x[c:2 * c] + x[2 * c:3 * c] + x[3 * c:4 * c]


def _gla_kernel(qk_ref, v_ref, glo_ref, wgk_ref, bgk_ref, tri_ref, h32_ref, h64_ref, causal_ref,
                o_ref, st_sc, *, n_chunks):
    @pl.when(pl.program_id(0) == 0)
    def _():
        st_sc[...] = jnp.zeros(st_sc.shape, F32)

    c = CHUNK
    tri = tri_ref[...]
    h32 = h32_ref[...]
    h64 = h64_ref[...]
    causal = causal_ref[...] != 0
    nb = qk_ref.shape[0]
    items = [(b, slice(j * c, (j + 1) * c)) for j in range(n_chunks) for b in range(nb)]
    pre = [_dot(_bf(glo_ref[b]), wgk_ref[...]) + bgk_ref[...] for b in range(nb)]
    gk = [-_softplus(-pre[b][rows, :]) / GLA_GATE_NORM for b, rows in items]
    q = [qk_ref[b, rows, :GLA_QK_WIDTH] for b, rows in items]
    k = [qk_ref[b, rows, GLA_QK_WIDTH:] for b, rows in items]
    v = [v_ref[b, rows, :] for b, rows in items]
    bc = _each(lambda x: _exact_left(tri, x), gk)
    b_last = _each(lambda x: x[c - 1:c, :], bc)
    q_t = _each(lambda q_, b_: q_ * (GLA_DK ** -0.5) * jnp.exp(b_), q, bc)
    k_t = _each(lambda k_, b_: k_ * jnp.exp(-b_), k, bc)
    k_end = _each(lambda k_, bl, b_: k_ * jnp.exp(bl - b_), k, b_last, bc)
    a = _each(lambda q_, k_: jnp.where(causal, _dot_nt(_bf(q_), _bf(_stack(k_, h32))), 0.0), q_t, k_t)
    o_intra = _each(lambda a_, v_: _dot(_bf(a_), _bf(_stack(v_, h64))), a, v)
    upd = _each(lambda v_, ke: _dot_tn(_bf(v_), _bf(ke)) * h32, v, k_end)
    decay = _each(jnp.exp, b_last)
    st = [st_sc[b] for b in range(nb)]
    for i, (b, rows) in enumerate(items):
        o_ref[b, rows, :] = o_intra[i] + _dot_nt(_bf(q_t[i]), _bf(st[b]))
        st[b] = st[b] * decay[i] + upd[i]
    for b in range(nb):
        st_sc[b] = st[b]


def _gla(pb, wgk, bgk, masks, *, batch, seq, tc):
    t = pb.shape[0]
    const = lambda s: (0, 0)
    pb3 = pb.reshape(batch, seq, pb.shape[1])
    o = pl.pallas_call(
        functools.partial(_gla_kernel, n_chunks=tc // CHUNK),
        grid=(seq // tc,),
        in_specs=[pl.BlockSpec((batch, tc, 2 * GLA_QK_WIDTH), lambda s: (0, s, 0)),
                  pl.BlockSpec((batch, tc, GLA_WIDTH), lambda s: (0, s, 1)),
                  pl.BlockSpec((batch, tc, LANES), lambda s: (0, s, 6)),
                  pl.BlockSpec(wgk.shape, const),
                  pl.BlockSpec(bgk.shape, const),
                  pl.BlockSpec(masks["tri"].shape, const),
                  pl.BlockSpec(masks["head32"].shape, const),
                  pl.BlockSpec(masks["head64"].shape, const),
                  pl.BlockSpec(masks["causal"].shape, const)],
        out_specs=pl.BlockSpec((batch, tc, GLA_WIDTH), lambda s: (0, s, 0)),
        out_shape=jax.ShapeDtypeStruct((batch, seq, GLA_WIDTH), F32),
        scratch_shapes=[pltpu.VMEM((batch, GLA_WIDTH, GLA_QK_WIDTH), F32)],
        compiler_params=_params("arbitrary"),
        name="gla",
    )(pb3, pb3, pb3, wgk, bgk, masks["tri"], masks["head32"], masks["head64"], masks["causal"])
    return o.reshape(t, GLA_WIDTH)


def _rwkv_prep_kernel(pc_ref, halo_ref, mu_ref, w0_ref, w2_ref, a0_ref, a2_ref, g2_ref, kk_ref, ka_ref,
                      rk_ref, grp_ref, r_out, lw_out, k_out, v_out, kk_out, b_out, g_out, bonus_out,
                      *, blocks_per_seq):
    xc = pc_ref[...]
    tm = xc.shape[0]
    first = (pl.program_id(0) % blocks_per_seq) == 0
    last_prev = jnp.where(first, 0.0, halo_ref[SUBLANES - 1:SUBLANES, :])
    row = lax.broadcasted_iota(jnp.int32, xc.shape, 0)
    prev = jnp.where(row == 0, last_prev, pltpu.roll(xc, 1, 0))
    xm = xc + (prev - xc) * mu_ref[...]
    w3 = RWKV_WIDTH
    r = xm[:, 0:w3]
    k = xm[:, w3:2 * w3]
    v = xm[:, 2 * w3:3 * w3]
    wa = xm[:, 3 * w3:3 * w3 + LANES]
    g_lo = xm[:, 3 * w3 + LANES:]
    w = -_softplus(-(w0_ref[...] + _dot(_bf(jnp.tanh(wa)), w2_ref[...]))) - 0.5
    a = _sigmoid(a0_ref[...] + _dot(_bf(wa), a2_ref[...]))
    g = _dot(_bf(_sigmoid(g_lo)), g2_ref[...])
    grp = grp_ref[...]
    kk = k * kk_ref[...]
    norm = jnp.sqrt(_exact_right(kk * kk, grp))
    kk = kk / jnp.maximum(norm, 1e-12)
    k2 = k * (1.0 + (a - 1.0) * ka_ref[...])
    r_out[...] = r
    lw_out[...] = -jnp.exp(w)
    k_out[...] = k2
    v_out[...] = v
    kk_out[...] = kk
    b_out[...] = kk * a
    g_out[...] = g
    bonus_out[...] = _exact_right(r * k2 * rk_ref[...], grp) * v


def _rwkv_prep(pc, mu, w0, w2p, a0, a2p, g2, k_k, k_a, r_k, grp, *, tm, seq):
    t = pc.shape[0]
    const = lambda i: (0, 0)
    hb = tm // SUBLANES
    out = jax.ShapeDtypeStruct((t, RWKV_WIDTH), F32)
    ospec = pl.BlockSpec((tm, RWKV_WIDTH), lambda i: (i, 0))
    small = [mu, w0, w2p, a0, a2p, g2, k_k, k_a, r_k, grp]
    return pl.pallas_call(
        functools.partial(_rwkv_prep_kernel, blocks_per_seq=seq // tm),
        grid=(t // tm,),
        in_specs=[pl.BlockSpec((tm, PC_W), lambda i: (i, 0)),
                  pl.BlockSpec((SUBLANES, PC_W), lambda i: (jnp.maximum(i * hb - 1, 0), 0))]
                 + [pl.BlockSpec(a.shape, const) for a in small],
        out_specs=[ospec] * 8,
        out_shape=[out] * 8,
        compiler_params=_params("parallel"),
        name="rwkv_prep",
    )(pc, pc, *small)


def _each(fn, *lists):
    return [fn(*xs) for xs in zip(*lists)]


def _rwkv_chunks(r, lw, k2, v, kk, bvec, s, tri, h64, strict, causal):
    c = CHUNK
    g = _each(lambda x: _exact_left(tri, x), lw)
    g_last = _each(lambda x: x[c - 1:c, :], g)
    e_neg = _each(lambda x: jnp.exp(-x), g)
    a_t = _each(lambda kk_, g_, lw_: -kk_ * jnp.exp(g_ - lw_), kk, g, lw)
    r_t = _each(lambda r_, g_: r_ * jnp.exp(g_), r, g)
    b_t = _each(jnp.multiply, bvec, e_neg)
    k_t = _each(jnp.multiply, k2, e_neg)
    e_end = _each(lambda gl, g_: jnp.exp(gl - g_), g_last, g)
    b_end = _each(jnp.multiply, bvec, e_end)
    k_end = _each(jnp.multiply, k2, e_end)

    a_s = _each(lambda x: _bf(_stack(x, h64)), a_t)
    bk_s = _each(lambda b_, k_: _bf(jnp.concatenate([_stack(b_, h64), _stack(k_, h64)], axis=0)), b_t, k_t)
    v_s = _each(lambda x: _bf(_stack(x, h64)), v)
    ab_ak = _each(_dot_nt, a_s, bk_s)
    l1 = _each(lambda x: jnp.where(strict, x[:, :4 * c], 0.0), ab_ak)
    a_ak = _each(lambda x: jnp.where(strict, x[:, 4 * c:], 0.0), ab_ak)
    rb_rk = _each(lambda r_, bk: _dot_nt(_bf(r_), bk), r_t, bk_s)
    a_rb = _each(lambda x: jnp.where(causal, x[:, :4 * c], 0.0), rb_rk)
    a_rk = _each(lambda x: jnp.where(causal, x[:, 4 * c:], 0.0), rb_rk)

    shape = l1[0].shape
    eye = (lax.broadcasted_iota(jnp.int32, shape, 0) == lax.broadcasted_iota(jnp.int32, shape, 1)).astype(F32)
    t_inv = _each(lambda x: eye + x, l1)
    lp = l1
    for _ in range(5):
        lp = _each(lambda x: _dot(_bf(x), _bf(x)), lp)
        t_inv = _each(lambda t, p: t + _dot(_bf(t), _bf(p)), t_inv, lp)
    t_b = _each(_bf, t_inv)
    av = _each(lambda a, vs: _bf(_dot(_bf(a), vs)), a_ak, v_s)
    w_a = _each(lambda t, a: _unstack(_dot(t, a)), t_b, a_s)
    u_v = _each(lambda t, x: _unstack(_dot(t, x)), t_b, av)

    s_b = _each(_bf, s)
    u = _each(lambda w, sb, uv_: _dot_nt(_bf(w), sb) + uv_, w_a, s_b, u_v)
    y = _each(lambda r_, sb, arb, u_, ark, vs: _dot_nt(_bf(r_), sb) + _dot(_bf(arb), _bf(_stack(u_, h64)))
              + _dot(_bf(ark), vs), r_t, s_b, a_rb, u, a_rk, v_s)
    uv = _each(lambda u_, v_: _bf(jnp.concatenate([u_, v_], axis=0)), u, v)
    bk_end = _each(lambda b_, k_: _bf(jnp.concatenate([b_, k_], axis=0)), b_end, k_end)
    s_new = _each(lambda s_, gl, x, z: s_ * jnp.exp(gl) + _dot_tn(x, z) * h64, s, g_last, uv, bk_end)
    return y, s_new


def _rwkv_kernel(r_ref, lw_ref, k_ref, v_ref, kk_ref, b_ref, tri_ref, h64_ref, strict_ref, causal_ref,
                 y_ref, s_sc):
    @pl.when(pl.program_id(0) == 0)
    def _():
        s_sc[...] = jnp.zeros(s_sc.shape, F32)

    tri = tri_ref[...]
    h64 = h64_ref[...]
    strict = strict_ref[...] != 0
    causal = causal_ref[...] != 0
    nb = r_ref.shape[0]
    per_seq = lambda ref: [ref[b] for b in range(nb)]
    y, s_new = _rwkv_chunks(per_seq(r_ref), per_seq(lw_ref), per_seq(k_ref), per_seq(v_ref), per_seq(kk_ref),
                            per_seq(b_ref), per_seq(s_sc), tri, h64, strict, causal)
    for b in range(nb):
        y_ref[b] = y[b]
        s_sc[b] = s_new[b]


def _rwkv(r, lw, k2, v, kk, bvec, masks, *, batch, seq):
    t = r.shape[0]
    const = lambda s: (0, 0)
    spec = pl.BlockSpec((batch, CHUNK, RWKV_WIDTH), lambda s: (0, s, 0))
    seqs = [a.reshape(batch, seq, RWKV_WIDTH) for a in (r, lw, k2, v, kk, bvec)]
    y = pl.pallas_call(
        _rwkv_kernel,
        grid=(seq // CHUNK,),
        in_specs=[spec] * 6 + [pl.BlockSpec(masks["tri"].shape, const),
                               pl.BlockSpec(masks["head64"].shape, const),
                               pl.BlockSpec(masks["strict"].shape, const),
                               pl.BlockSpec(masks["causal"].shape, const)],
        out_specs=spec,
        out_shape=jax.ShapeDtypeStruct((batch, seq, RWKV_WIDTH), F32),
        scratch_shapes=[pltpu.VMEM((batch, RWKV_WIDTH, RWKV_WIDTH), F32)],
        compiler_params=_params("arbitrary"),
        name="rwkv",
    )(*seqs, masks["tri"], masks["head64"], masks["strict"], masks["causal"])
    return y.reshape(t, RWKV_WIDTH)


def _mix_out_kernel(x_ref, oa_ref, ob_ref, gout_ref, yc_ref, bonus_ref, gate_ref, na_ref, nb_ref,
                    lng_ref, lnb_ref, grp_ref, wa_ref, wb_ref, wc_ref, o_ref):
    grp = grp_ref[...]
    ya = _rms(oa_ref[...], na_ref[...])
    ob = ob_ref[...]
    ms = _exact_right(ob * ob, grp) * (1.0 / GLA_DV)
    gout = gout_ref[...]
    yb = ob * lax.rsqrt(ms + NORM_EPS) * nb_ref[...] * (gout * _sigmoid(gout))
    y = yc_ref[...]
    mean = _exact_right(y, grp) * (1.0 / RWKV_N)
    d = y - mean
    var = _exact_right(d * d, grp) * (1.0 / RWKV_N)
    yc = (d * lax.rsqrt(var + RWKV_LN_EPS) * lng_ref[...] + lnb_ref[...] + bonus_ref[...]) * gate_ref[...]
    o_ref[...] = (x_ref[...] + _dot(_bf(ya), wa_ref[...]) + _dot(_bf(yb), wb_ref[...])
                  + _dot(_bf(yc), wc_ref[...]))


def _mix_out(x, oa, ob, pb, yc, bonus, gate, na, nb, lng, lnb, grp, wa, wb, wc, *, tm):
    t, d = x.shape
    const = lambda i: (0, 0)
    row = lambda w: pl.BlockSpec((tm, w), lambda i: (i, 0))
    small = [na, nb, lng, lnb, grp, wa, wb, wc]
    return pl.pallas_call(
        _mix_out_kernel,
        grid=(t // tm,),
        in_specs=[row(d), row(MLA_WIDTH), row(GLA_WIDTH),
                  pl.BlockSpec((tm, GLA_WIDTH), lambda i: (i, 2)),
                  row(RWKV_WIDTH), row(RWKV_WIDTH), row(RWKV_WIDTH)]
                 + [pl.BlockSpec(a.shape, const) for a in small],
        out_specs=row(d),
        out_shape=jax.ShapeDtypeStruct((t, d), F32),
        compiler_params=_params("parallel"),
        name="mix_out",
    )(x, oa, ob, pb, yc, bonus, gate, *small)


def _causal_conv3(u, prev, cw, cb):
    row = lax.broadcasted_iota(jnp.int32, u.shape, 0)
    u1 = jnp.where(row == 0, prev[SUBLANES - 1:SUBLANES, :], pltpu.roll(u, 1, 0))
    u2 = jnp.where(row == 0, prev[SUBLANES - 2:SUBLANES - 1, :],
                   jnp.where(row == 1, prev[SUBLANES - 1:SUBLANES, :], pltpu.roll(u, 2, 0)))
    return cw[0:1, :] * u2 + cw[1:2, :] * u1 + cw[2:3, :] * u + cb


def _ffn_kernel(x_ref, g_ref, wu_ref, cw_ref, cb_ref, wd_ref, fg_ref, o_ref, cg_sc, cv_sc,
                *, d_ff, tf, final_norm):
    x = x_ref[...]
    h = _bf(_rms(x, g_ref[...]))
    ts = h.shape[0]
    first = pl.program_id(1) == 0
    acc = None
    for c in range(d_ff // tf):
        gcol = slice(c * tf, (c + 1) * tf)
        vcol = slice(d_ff + c * tf, d_ff + (c + 1) * tf)
        ug = _dot(h, wu_ref[:, gcol])
        uv = _dot(h, wu_ref[:, vcol])
        gate = _causal_conv3(ug, jnp.where(first, 0.0, cg_sc[c]), cw_ref[:, gcol], cb_ref[:, gcol])
        val = _causal_conv3(uv, jnp.where(first, 0.0, cv_sc[c]), cw_ref[:, vcol], cb_ref[:, vcol])
        cg_sc[c] = ug[ts - SUBLANES:, :]
        cv_sc[c] = uv[ts - SUBLANES:, :]
        part = _dot(_bf(gate * _sigmoid(gate) * val), wd_ref[gcol, :])
        acc = part if acc is None else acc + part
    y = x + acc
    if final_norm:
        y = _rms(y, fg_ref[...])
    o_ref[...] = y


def _ffn(x, g, w_up, conv_w, conv_b, w_down, final_g, *, batch, seq, ts, tf, final_norm):
    t, d = x.shape
    d_ff = w_down.shape[0]
    nsb = seq // ts
    row = lambda b, s: (b * nsb + s, 0)
    const = lambda b, s: (0, 0)
    resident = lambda a: pl.BlockSpec(a.shape, const, pipeline_mode=pl.Buffered(1))
    return pl.pallas_call(
        functools.partial(_ffn_kernel, d_ff=d_ff, tf=tf, final_norm=final_norm),
        grid=(batch, nsb),
        in_specs=[pl.BlockSpec((ts, d), row),
                  pl.BlockSpec((1, d), const),
                  resident(w_up), resident(conv_w), resident(conv_b), resident(w_down),
                  pl.BlockSpec((1, d), const)],
        out_specs=pl.BlockSpec((ts, d), row),
        out_shape=jax.ShapeDtypeStruct((t, d), F32),
        scratch_shapes=[pltpu.VMEM((d_ff // tf, SUBLANES, tf), F32),
                        pltpu.VMEM((d_ff // tf, SUBLANES, tf), F32)],
        compiler_params=_params("parallel", "arbitrary"),
        name="ffn",
    )(x, g, w_up, conv_w, conv_b, w_down, final_g)


def _rot_cols(w):
    half = w.shape[-1] // 2
    return jnp.concatenate([-w[..., half:], w[..., :half]], axis=-1)


def _regroup_w_in(w_in):
    d = w_in.shape[0]
    a0 = 0
    b0 = MLA_COLS
    c0 = MLA_COLS + GLA_COLS
    k_pe = w_in[:, MLA_Q_LORA + MLA_KV_LORA:MLA_COLS]
    gq = w_in[:, b0:b0 + GLA_QK_WIDTH]
    gk = w_in[:, b0 + GLA_QK_WIDTH:b0 + 2 * GLA_QK_WIDTH]
    gv = w_in[:, b0 + 2 * GLA_QK_WIDTH:b0 + 2 * GLA_QK_WIDTH + GLA_WIDTH]
    glo = w_in[:, b0 + 2 * GLA_QK_WIDTH + GLA_WIDTH:b0 + 2 * GLA_QK_WIDTH + GLA_WIDTH + GLA_GATE_RANK]
    gout = w_in[:, b0 + 2 * GLA_QK_WIDTH + GLA_WIDTH + GLA_GATE_RANK:c0]
    pad = jnp.zeros((d, LANES - GLA_GATE_RANK), w_in.dtype)
    cols = [w_in[:, a0:MLA_COLS], _rot_cols(k_pe), gq, gk, gv, gout, glo, pad, w_in[:, c0:]]
    return _bf(jnp.concatenate(cols, axis=1))


def _regroup_w_uq(w_uq):
    r = w_uq.shape[0]
    w = w_uq.reshape(r, MLA_HEADS, MLA_QK)
    nope = w[:, :, :MLA_NOPE]
    pe = w[:, :, MLA_NOPE:]
    return _bf(jnp.concatenate([nope, pe, _rot_cols(pe)], axis=-1).reshape(r, MLA_HEADS * QK_HEAD_W))


def _regroup_w_ukv(w_ukv):
    r = w_ukv.shape[0]
    w = w_ukv.reshape(r, MLA_HEADS, MLA_NOPE + MLA_V)
    k_nope = w[:, :, :MLA_NOPE].reshape(r, MLA_HEADS * MLA_NOPE)
    v = w[:, :, MLA_NOPE:].reshape(r, MLA_HEADS * MLA_V)
    return _bf(jnp.concatenate([k_nope, v], axis=1))


def _rope_table(seq):
    inv = 1.0 / (ROPE_THETA ** (jnp.arange(0, MLA_ROPE, 2, dtype=F32) / MLA_ROPE))
    ang = jnp.arange(seq, dtype=F32)[:, None] * inv[None, :]
    cos, sin = jnp.cos(ang), jnp.sin(ang)
    return jnp.concatenate([cos, cos, sin, sin], axis=1)


def _row(v):
    return v.reshape(1, -1).astype(F32)


def _pad_rows(w, top, total):
    return jnp.concatenate([jnp.zeros((top, w.shape[1]), w.dtype), w,
                            jnp.zeros((total - top - w.shape[0], w.shape[1]), w.dtype)], axis=0)


def _tiles(seq):
    return {
        "tm": min(512, seq),
        "tk": min(512, seq // 2),
        "tc": min(256, seq),
        "ts": min(512, seq),
        "tf": 256,
    }


def kernel(x, ln1_g, w_in, mla_q_norm_g, mla_w_uq, mla_kv_norm_g, mla_w_ukv, mla_out_norm_g, gla_w_gk, gla_b_gk, gla_norm_g, rwkv_mu, rwkv_w0, rwkv_w2, rwkv_a0, rwkv_a2, rwkv_g2, rwkv_k_k, rwkv_k_a, rwkv_r_k, rwkv_ln_g, rwkv_ln_b, w_out, ln2_g, ffn_w_up, ffn_conv_w, ffn_conv_b, ffn_w_down, final_g):
    batch, seq, d_model = x.shape
    depth = w_in.shape[0]
    tl = _tiles(seq)
    masks = _mask_inputs()
    grp = masks["group64"]
    rope = _rope_table(seq)
    xt = x.reshape(batch * seq, d_model)
    for l in range(depth):
        pa, pb, pc = _in_proj(xt, _row(ln1_g[l]), _regroup_w_in(w_in[l]), tm=tl["tm"])
        q, k, v = _mla_prep(pa, _row(mla_q_norm_g[l]), _row(mla_kv_norm_g[l]), _regroup_w_uq(mla_w_uq[l]),
                            _regroup_w_ukv(mla_w_ukv[l]), rope, tm=tl["tm"], seq=seq)
        o_mla = _flash(q, k, v, batch=batch, seq=seq, tk=tl["tk"])
        wgk = _bf(_pad_rows(gla_w_gk[l], 0, LANES))
        o_gla = _gla(pb, wgk, _row(gla_b_gk[l]), masks, batch=batch, seq=seq, tc=tl["tc"])
        w2p = _bf(_pad_rows(rwkv_w2[l], 0, LANES))
        a2p = _bf(_pad_rows(rwkv_a2[l], RWKV_DECAY_RANK, LANES))
        r, lw, k2, vv, kk, bvec, gate, bonus = _rwkv_prep(
            pc, _row(rwkv_mu[l]), _row(rwkv_w0[l]), w2p, _row(rwkv_a0[l]), a2p, _bf(rwkv_g2[l]),
            _row(rwkv_k_k[l]), _row(rwkv_k_a[l]), _row(rwkv_r_k[l]), grp, tm=tl["tm"], seq=seq)
        y_rwkv = _rwkv(r, lw, k2, vv, kk, bvec, masks, batch=batch, seq=seq)
        wo = _bf(w_out[l])
        xt = _mix_out(xt, o_mla, o_gla, pb, y_rwkv, bonus, gate, _row(mla_out_norm_g[l]),
                      _row(jnp.tile(gla_norm_g[l], GLA_HEADS)), _row(rwkv_ln_g[l]), _row(rwkv_ln_b[l]), grp,
                      wo[:MLA_WIDTH], wo[MLA_WIDTH:MLA_WIDTH + GLA_WIDTH], wo[MLA_WIDTH + GLA_WIDTH:],
                      tm=tl["tm"])
        xt = _ffn(xt, _row(ln2_g[l]), _bf(ffn_w_up[l]), ffn_conv_w[l].astype(F32), _row(ffn_conv_b[l]),
                  _bf(ffn_w_down[l]), _row(final_g), batch=batch, seq=seq, ts=tl["ts"], tf=tl["tf"],
                  final_norm=(l == depth - 1))
    return xt.reshape(batch, seq, d_model)
```

```python
import functools
import math

import numpy as np
import jax
import jax.numpy as jnp
from jax import lax
from jax.experimental import pallas as pl
from jax.experimental.pallas import tpu as pltpu

F32 = jnp.float32
BF16 = jnp.bfloat16

MLA_HEADS = 4
MLA_NOPE = 128
MLA_ROPE = 64
MLA_V = 128
MLA_QK = MLA_NOPE + MLA_ROPE
MLA_Q_LORA = 384
MLA_KV_LORA = 256
MLA_WIDTH = MLA_HEADS * MLA_V
ROPE_THETA = 10000.0
GLA_HEADS = 4
GLA_DK = 32
GLA_DV = 64
GLA_QK_WIDTH = GLA_HEADS * GLA_DK
GLA_WIDTH = GLA_HEADS * GLA_DV
GLA_GATE_RANK = 16
GLA_GATE_NORM = 16.0
RWKV_HEADS = 4
RWKV_N = 64
RWKV_WIDTH = RWKV_HEADS * RWKV_N
RWKV_DECAY_RANK = 64
RWKV_A_RANK = 64
RWKV_GATE_RANK = 128
RWKV_LN_EPS = 64e-5
MLA_COLS = MLA_Q_LORA + MLA_KV_LORA + MLA_ROPE
GLA_COLS = 2 * GLA_QK_WIDTH + GLA_WIDTH + GLA_GATE_RANK + GLA_WIDTH
RWKV_COLS = 3 * RWKV_WIDTH + RWKV_DECAY_RANK + RWKV_A_RANK + RWKV_GATE_RANK
NORM_EPS = 1e-6
CONV_WIDTH = 3

LANES = 128
SUBLANES = 8
VMEM_LIMIT = 56 * 1024 * 1024

PA_W = MLA_Q_LORA + MLA_KV_LORA + 2 * MLA_ROPE
PB_W = 2 * GLA_QK_WIDTH + 2 * GLA_WIDTH + LANES
PC_W = RWKV_COLS
QK_HEAD_W = 2 * LANES

LOG2E = math.log2(math.e)

CHUNK = 64


def _dot(a, b):
    return lax.dot_general(a, b, (((1,), (0,)), ((), ())), preferred_element_type=F32)


def _dot_nt(a, b):
    return lax.dot_general(a, b, (((1,), (1,)), ((), ())), preferred_element_type=F32)


def _dot_tn(a, b):
    return lax.dot_general(a, b, (((0,), (0,)), ((), ())), preferred_element_type=F32)


def _bf(x):
    return x.astype(BF16)


def _split_terms(x, n):
    terms = []
    rem = x
    for _ in range(n):
        t = rem.astype(BF16)
        terms.append(t)
        rem = rem - t.astype(F32)
    return terms


def _exact_left(m, x, n=3):
    out = None
    for t in _split_terms(x, n):
        y = _dot(m, t)
        out = y if out is None else out + y
    return out


def _exact_right(x, m, n=2):
    out = None
    for t in _split_terms(x, n):
        y = _dot(t, m)
        out = y if out is None else out + y
    return out


def _rms(x, g, eps=NORM_EPS):
    return x * lax.rsqrt(jnp.mean(x * x, axis=-1, keepdims=True) + eps) * g


def _sigmoid(x):
    return 1.0 / (1.0 + jnp.exp(-x))


def _softplus(x):
    return jnp.maximum(x, 0.0) + jnp.log(1.0 + jnp.exp(-jnp.abs(x)))


def _params(*sem):
    return pltpu.CompilerParams(dimension_semantics=sem, vmem_limit_bytes=VMEM_LIMIT)


def _in_proj_kernel(x_ref, g_ref, w_ref, pa_ref, pb_ref, pc_ref, *, n_step):
    h = _bf(_rms(x_ref[...], g_ref[...]))
    outs = ((pa_ref, 0, PA_W), (pb_ref, PA_W, PB_W), (pc_ref, PA_W + PB_W, PC_W))
    for ref, base, width in outs:
        for c in range(0, width, n_step):
            w = min(n_step, width - c)
            ref[:, c:c + w] = _dot(h, w_ref[:, base + c:base + c + w])


def _in_proj(x, g, w, *, tm):
    t, d = x.shape
    n = w.shape[1]
    return pl.pallas_call(
        functools.partial(_in_proj_kernel, n_step=512),
        grid=(t // tm,),
        in_specs=[pl.BlockSpec((tm, d), lambda i: (i, 0)),
                  pl.BlockSpec((1, d), lambda i: (0, 0)),
                  pl.BlockSpec((d, n), lambda i: (0, 0))],
        out_specs=[pl.BlockSpec((tm, PA_W), lambda i: (i, 0)),
                   pl.BlockSpec((tm, PB_W), lambda i: (i, 0)),
                   pl.BlockSpec((tm, PC_W), lambda i: (i, 0))],
        out_shape=[jax.ShapeDtypeStruct((t, PA_W), F32),
                   jax.ShapeDtypeStruct((t, PB_W), F32),
                   jax.ShapeDtypeStruct((t, PC_W), F32)],
        compiler_params=_params("parallel"),
        name="in_proj",
    )(x, g, w)


def _mla_prep_kernel(pa_ref, gq_ref, gkv_ref, wq_ref, wkv_ref, rope_ref, q_ref, k_ref, v_ref):
    pa = pa_ref[...]
    rope = rope_ref[...]
    cq = _bf(_rms(pa[:, :MLA_Q_LORA], gq_ref[...]))
    q = _dot(cq, wq_ref[...]) * (MLA_QK ** -0.5 * LOG2E)
    for h in range(MLA_HEADS):
        b = h * QK_HEAD_W
        q_ref[:, b:b + LANES] = _bf(q[:, b:b + LANES])
        q_ref[:, b + LANES:b + QK_HEAD_W] = _bf(q[:, b + LANES:b + QK_HEAD_W] * rope)
    ckv = _bf(_rms(pa[:, MLA_Q_LORA:MLA_Q_LORA + MLA_KV_LORA], gkv_ref[...]))
    kv = _dot(ckv, wkv_ref[...])
    kr = pa[:, MLA_Q_LORA + MLA_KV_LORA:] * rope
    kr = _bf(kr + pltpu.roll(kr, MLA_ROPE, 1))
    for h in range(MLA_HEADS):
        b = h * QK_HEAD_W
        k_ref[:, b:b + LANES] = _bf(kv[:, h * MLA_NOPE:(h + 1) * MLA_NOPE])
        k_ref[:, b + LANES:b + QK_HEAD_W] = kr
    ones = jnp.ones((pa.shape[0], MLA_V), BF16)
    for h in range(MLA_HEADS):
        b = h * 2 * MLA_V
        v_ref[:, b:b + MLA_V] = _bf(kv[:, (MLA_HEADS + h) * MLA_NOPE:(MLA_HEADS + h + 1) * MLA_NOPE])
        v_ref[:, b + MLA_V:b + 2 * MLA_V] = ones


def _mla_prep(pa, gq, gkv, wq, wkv, rope, *, tm, seq):
    t = pa.shape[0]
    n_seq_blocks = seq // tm
    const = lambda i: (0, 0)
    return pl.pallas_call(
        _mla_prep_kernel,
        grid=(t // tm,),
        in_specs=[pl.BlockSpec((tm, PA_W), lambda i: (i, 0)),
                  pl.BlockSpec(gq.shape, const),
                  pl.BlockSpec(gkv.shape, const),
                  pl.BlockSpec(wq.shape, const),
                  pl.BlockSpec(wkv.shape, const),
                  pl.BlockSpec((tm, LANES), lambda i: (i % n_seq_blocks, 0))],
        out_specs=[pl.BlockSpec((tm, MLA_HEADS * QK_HEAD_W), lambda i: (i, 0)),
                   pl.BlockSpec((tm, MLA_HEADS * QK_HEAD_W), lambda i: (i, 0)),
                   pl.BlockSpec((tm, 2 * MLA_WIDTH), lambda i: (i, 0))],
        out_shape=[jax.ShapeDtypeStruct((t, MLA_HEADS * QK_HEAD_W), BF16),
                   jax.ShapeDtypeStruct((t, MLA_HEADS * QK_HEAD_W), BF16),
                   jax.ShapeDtypeStruct((t, 2 * MLA_WIDTH), BF16)],
        compiler_params=_params("parallel"),
        name="mla_prep",
    )(pa, gq, gkv, wq, wkv, rope)


def _flash_kernel(q_ref, k_ref, v_ref, o_ref, m_sc, acc_sc, s_sc, *, tk):
    qi = pl.program_id(2)
    m_sc[...] = jnp.full(m_sc.shape, -jnp.inf, F32)
    acc_sc[...] = jnp.zeros(acc_sc.shape, F32)

    def kv_rows(j):
        return pl.ds(pl.multiple_of(j * tk, tk), tk)

    def scores(sub, j):
        return _dot_nt(q_ref[sub * tk:(sub + 1) * tk, :], k_ref[kv_rows(j), :])

    def accumulate(sub, j, s, masked):
        rows = slice(sub * tk, (sub + 1) * tk)
        if masked:
            row = lax.broadcasted_iota(jnp.int32, s.shape, 0)
            col = lax.broadcasted_iota(jnp.int32, s.shape, 1)
            s = jnp.where(row >= col, s, -jnp.inf)
        m_prev = m_sc[rows, :]
        m_next = jnp.maximum(m_prev, jnp.max(s, axis=1, keepdims=True))
        alpha = jnp.exp2(m_prev - m_next)
        p = jnp.exp2(s - jnp.concatenate([m_next] * (tk // LANES), axis=1))
        acc_sc[rows, :] = (acc_sc[rows, :] * jnp.concatenate([alpha, alpha], axis=1)
                           + _dot(_bf(p), v_ref[kv_rows(j), :]))
        m_sc[rows, :] = m_next

    for sub in range(2):
        s_sc[0, sub] = scores(sub, 0)

    def body(jj, carry):
        j = 2 * jj
        for slot in range(2):
            for sub in range(2):
                s = s_sc[slot, sub]
                s_sc[1 - slot, sub] = scores(sub, j + slot + 1)
                accumulate(sub, j + slot, s, False)
        return carry

    lax.fori_loop(0, qi, body, 0)
    j = 2 * qi
    s_last = scores(1, j + 1)
    accumulate(0, j, s_sc[0, 0], True)
    accumulate(1, j, s_sc[0, 1], False)
    accumulate(1, j + 1, s_last, True)
    acc = acc_sc[...]
    o_ref[...] = _bf(acc[:, :MLA_V] / acc[:, MLA_V:])


def _flash(q, k, v1, *, batch, seq, tk):
    tq = 2 * tk
    nq = seq // tq
    return pl.pallas_call(
        functools.partial(_flash_kernel, tk=tk),
        grid=(batch, MLA_HEADS, nq),
        in_specs=[pl.BlockSpec((tq, QK_HEAD_W), lambda b, h, i: (b * nq + i, h)),
                  pl.BlockSpec((seq, QK_HEAD_W), lambda b, h, i: (b, h)),
                  pl.BlockSpec((seq, 2 * MLA_V), lambda b, h, i: (b, h))],
        out_specs=pl.BlockSpec((tq, MLA_V), lambda b, h, i: (b * nq + i, h)),
        out_shape=jax.ShapeDtypeStruct((batch * seq, MLA_WIDTH), BF16),
        scratch_shapes=[pltpu.VMEM((tq, LANES), F32),
                        pltpu.VMEM((tq, 2 * MLA_V), F32),
                        pltpu.VMEM((2, 2, tk, tk), F32)],
        compiler_params=_params("parallel", "parallel", "arbitrary"),
        name="flash",
    )(q, k, v1)


def _np_masks():
    c = CHUNK
    r4 = np.arange(4 * c)
    tri = (np.arange(c)[:, None] >= np.arange(c)[None, :])
    m = {
        "tri": tri,
        "head64": (r4[:, None] // c) == (np.arange(RWKV_WIDTH)[None, :] // RWKV_N),
        "head32": (r4[:, None] // c) == (np.arange(GLA_QK_WIDTH)[None, :] // GLA_DK),
        "strict": (r4[:, None] % c) > (r4[None, :] % c),
        "causal": np.arange(c)[:, None] >= (r4[None, :] % c),
    }
    return m


def _mask_inputs():
    m = _np_masks()
    return {
        "tri": jnp.asarray(m["tri"], BF16),
        "head64": jnp.asarray(m["head64"], F32),
        "head32": jnp.asarray(m["head32"], F32),
        "strict": jnp.asarray(m["strict"], F32),
        "causal": jnp.asarray(m["causal"], F32),
        "group64": jnp.asarray(m["head64"], BF16),
    }


def _stack(x, head_mask):
    return jnp.concatenate([x] * 4, axis=0) * head_mask


def _unstack(x):
    c = x.shape[0] // 4
    return x[0:c] + x[c:2 * c] + x[2 * c:3 * c] + x[3 * c:4 * c]


def _gla_kernel(qk_ref, v_ref, glo_ref, wgk_ref, bgk_ref, tri_ref, h32_ref, h64_ref, causal_ref,
                o_ref, st_sc, *, n_chunks):
    @pl.when(pl.program_id(0) == 0)
    def _():
        st_sc[...] = jnp.zeros(st_sc.shape, F32)

    c = CHUNK
    tri = tri_ref[...]
    h32 = h32_ref[...]
    h64 = h64_ref[...]
    causal = causal_ref[...] != 0
    nb = qk_ref.shape[0]
    items = [(b, slice(j * c, (j + 1) * c)) for j in range(n_chunks) for b in range(nb)]
    pre = [_dot(_bf(glo_ref[b]), wgk_ref[...]) + bgk_ref[...] for b in range(nb)]
    gk = [-_softplus(-pre[b][rows, :]) / GLA_GATE_NORM for b, rows in items]
    q = [qk_ref[b, rows, :GLA_QK_WIDTH] for b, rows in items]
    k = [qk_ref[b, rows, GLA_QK_WIDTH:] for b, rows in items]
    v = [v_ref[b, rows, :] for b, rows in items]
    bc = _each(lambda x: _exact_left(tri, x), gk)
    b_last = _each(lambda x: x[c - 1:c, :], bc)
    q_t = _each(lambda q_, b_: q_ * (GLA_DK ** -0.5) * jnp.exp(b_), q, bc)
    k_t = _each(lambda k_, b_: k_ * jnp.exp(-b_), k, bc)
    k_end = _each(lambda k_, bl, b_: k_ * jnp.exp(bl - b_), k, b_last, bc)
    a = _each(lambda q_, k_: jnp.where(causal, _dot_nt(_bf(q_), _bf(_stack(k_, h32))), 0.0), q_t, k_t)
    o_intra = _each(lambda a_, v_: _dot(_bf(a_), _bf(_stack(v_, h64))), a, v)
    upd = _each(lambda v_, ke: _dot_tn(_bf(v_), _bf(ke)) * h32, v, k_end)
    decay = _each(jnp.exp, b_last)
    st = [st_sc[b] for b in range(nb)]
    for i, (b, rows) in enumerate(items):
        o_ref[b, rows, :] = o_intra[i] + _dot_nt(_bf(q_t[i]), _bf(st[b]))
        st[b] = st[b] * decay[i] + upd[i]
    for b in range(nb):
        st_sc[b] = st[b]


def _gla(pb, wgk, bgk, masks, *, batch, seq, tc):
    t = pb.shape[0]
    const = lambda s: (0, 0)
    pb3 = pb.reshape(batch, seq, pb.shape[1])
    o = pl.pallas_call(
        functools.partial(_gla_kernel, n_chunks=tc // CHUNK),
        grid=(seq // tc,),
        in_specs=[pl.BlockSpec((batch, tc, 2 * GLA_QK_WIDTH), lambda s: (0, s, 0)),
                  pl.BlockSpec((batch, tc, GLA_WIDTH), lambda s: (0, s, 1)),
                  pl.BlockSpec((batch, tc, LANES), lambda s: (0, s, 6)),
                  pl.BlockSpec(wgk.shape, const),
                  pl.BlockSpec(bgk.shape, const),
                  pl.BlockSpec(masks["tri"].shape, const),
                  pl.BlockSpec(masks["head32"].shape, const),
                  pl.BlockSpec(masks["head64"].shape, const),
                  pl.BlockSpec(masks["causal"].shape, const)],
        out_specs=pl.BlockSpec((batch, tc, GLA_WIDTH), lambda s: (0, s, 0)),
        out_shape=jax.ShapeDtypeStruct((batch, seq, GLA_WIDTH), F32),
        scratch_shapes=[pltpu.VMEM((batch, GLA_WIDTH, GLA_QK_WIDTH), F32)],
        compiler_params=_params("arbitrary"),
        name="gla",
    )(pb3, pb3, pb3, wgk, bgk, masks["tri"], masks["head32"], masks["head64"], masks["causal"])
    return o.reshape(t, GLA_WIDTH)


def _rwkv_prep_kernel(pc_ref, halo_ref, mu_ref, w0_ref, w2_ref, a0_ref, a2_ref, g2_ref, kk_ref, ka_ref,
                      rk_ref, grp_ref, r_out, lw_out, k_out, v_out, kk_out, b_out, g_out, bonus_out,
                      *, blocks_per_seq):
    xc = pc_ref[...]
    tm = xc.shape[0]
    first = (pl.program_id(0) % blocks_per_seq) == 0
    last_prev = jnp.where(first, 0.0, halo_ref[SUBLANES - 1:SUBLANES, :])
    row = lax.broadcasted_iota(jnp.int32, xc.shape, 0)
    prev = jnp.where(row == 0, last_prev, pltpu.roll(xc, 1, 0))
    xm = xc + (prev - xc) * mu_ref[...]
    w3 = RWKV_WIDTH
    r = xm[:, 0:w3]
    k = xm[:, w3:2 * w3]
    v = xm[:, 2 * w3:3 * w3]
    wa = xm[:, 3 * w3:3 * w3 + LANES]
    g_lo = xm[:, 3 * w3 + LANES:]
    w = -_softplus(-(w0_ref[...] + _dot(_bf(jnp.tanh(wa)), w2_ref[...]))) - 0.5
    a = _sigmoid(a0_ref[...] + _dot(_bf(wa), a2_ref[...]))
    g = _dot(_bf(_sigmoid(g_lo)), g2_ref[...])
    grp = grp_ref[...]
    kk = k * kk_ref[...]
    norm = jnp.sqrt(_exact_right(kk * kk, grp))
    kk = kk / jnp.maximum(norm, 1e-12)
    k2 = k * (1.0 + (a - 1.0) * ka_ref[...])
    r_out[...] = _bf(r)
    lw_out[...] = -jnp.exp(w)
    k_out[...] = _bf(k2)
    v_out[...] = _bf(v)
    kk_out[...] = _bf(kk)
    b_out[...] = _bf(kk * a)
    g_out[...] = _bf(g)
    bonus_out[...] = _bf(_exact_right(r * k2 * rk_ref[...], grp) * v)


def _rwkv_prep(pc, mu, w0, w2p, a0, a2p, g2, k_k, k_a, r_k, grp, *, tm, seq):
    t = pc.shape[0]
    const = lambda i: (0, 0)
    hb = tm // SUBLANES
    out = lambda dt: jax.ShapeDtypeStruct((t, RWKV_WIDTH), dt)
    ospec = pl.BlockSpec((tm, RWKV_WIDTH), lambda i: (i, 0))
    small = [mu, w0, w2p, a0, a2p, g2, k_k, k_a, r_k, grp]
    return pl.pallas_call(
        functools.partial(_rwkv_prep_kernel, blocks_per_seq=seq // tm),
        grid=(t // tm,),
        in_specs=[pl.BlockSpec((tm, PC_W), lambda i: (i, 0)),
                  pl.BlockSpec((SUBLANES, PC_W), lambda i: (jnp.maximum(i * hb - 1, 0), 0))]
                 + [pl.BlockSpec(a.shape, const) for a in small],
        out_specs=[ospec] * 8,
        out_shape=[out(BF16), out(F32)] + [out(BF16)] * 6,
        compiler_params=_params("parallel"),
        name="rwkv_prep",
    )(pc, pc, *small)


def _each(fn, *lists):
    return [fn(*xs) for xs in zip(*lists)]


def _rwkv_pre(r, lw, k2, v, kk, bvec, tri, h64, strict, causal):
    c = CHUNK
    g = _each(lambda x: _exact_left(tri, x), lw)
    g_last = _each(lambda x: x[c - 1:c, :], g)
    e_neg = _each(lambda x: jnp.exp(-x), g)
    a_t = _each(lambda kk_, g_, lw_: -kk_ * jnp.exp(g_ - lw_), kk, g, lw)
    r_t = _each(lambda r_, g_: r_ * jnp.exp(g_), r, g)
    b_t = _each(jnp.multiply, bvec, e_neg)
    k_t = _each(jnp.multiply, k2, e_neg)
    e_end = _each(lambda gl, g_: jnp.exp(gl - g_), g_last, g)
    b_end = _each(jnp.multiply, bvec, e_end)
    k_end = _each(jnp.multiply, k2, e_end)

    a_s = _each(lambda x: _bf(_stack(x, h64)), a_t)
    bk_s = _each(lambda b_, k_: _bf(jnp.concatenate([_stack(b_, h64), _stack(k_, h64)], axis=0)), b_t, k_t)
    v_s = _each(lambda x: _bf(_stack(x, h64)), v)
    ab_ak = _each(_dot_nt, a_s, bk_s)
    l1 = _each(lambda x: jnp.where(strict, x[:, :4 * c], 0.0), ab_ak)
    a_ak = _each(lambda x: jnp.where(strict, x[:, 4 * c:], 0.0), ab_ak)
    rb_rk = _each(lambda r_, bk: _dot_nt(_bf(r_), bk), r_t, bk_s)
    a_rb = _each(lambda x: jnp.where(causal, x[:, :4 * c], 0.0), rb_rk)
    a_rk = _each(lambda x: jnp.where(causal, x[:, 4 * c:], 0.0), rb_rk)

    shape = l1[0].shape
    eye = (lax.broadcasted_iota(jnp.int32, shape, 0) == lax.broadcasted_iota(jnp.int32, shape, 1)).astype(F32)
    t_inv = _each(lambda x: eye + x, l1)
    lp = l1
    for _ in range(5):
        lp = _each(lambda x: _dot(_bf(x), _bf(x)), lp)
        t_inv = _each(lambda t, p: t + _dot(_bf(t), _bf(p)), t_inv, lp)
    t_b = _each(_bf, t_inv)
    av = _each(lambda a, vs: _bf(_dot(_bf(a), vs)), a_ak, v_s)
    w_a = _each(lambda t, a: _unstack(_dot(t, a)), t_b, a_s)
    u_v = _each(lambda t, x: _unstack(_dot(t, x)), t_b, av)

    y_v = _each(lambda ark, vs: _dot(_bf(ark), vs), a_rk, v_s)
    bk_end = _each(lambda b_, k_: _bf(jnp.concatenate([b_, k_], axis=0)), b_end, k_end)
    decay = _each(jnp.exp, g_last)
    return {"w_a": _each(_bf, w_a), "u_v": u_v, "r_t": _each(_bf, r_t), "a_rb": _each(_bf, a_rb), "y_v": y_v,
            "v": v, "bk_end": bk_end, "decay": decay}


def _rwkv_post(p, items, s, h64):
    pick = lambda name: [p[name][i] for i in items]
    s_b = _each(_bf, s)
    u = _each(lambda w, sb, uv_: _dot_nt(w, sb) + uv_, pick("w_a"), s_b, pick("u_v"))
    y = _each(lambda r_, sb, arb, u_, yv: _dot_nt(r_, sb) + _dot(arb, _bf(_stack(u_, h64))) + yv,
              pick("r_t"), s_b, pick("a_rb"), u, pick("y_v"))
    uv = _each(lambda u_, v_: _bf(jnp.concatenate([u_, v_], axis=0)), u, pick("v"))
    s_new = _each(lambda s_, d, x, z: s_ * d + _dot_tn(x, z) * h64, s, pick("decay"), uv, pick("bk_end"))
    return y, s_new


def _rwkv_kernel(r_ref, lw_ref, k_ref, v_ref, kk_ref, b_ref, tri_ref, h64_ref, strict_ref, causal_ref,
                 y_ref, s_sc, *, n_chunks):
    @pl.when(pl.program_id(0) == 0)
    def _():
        s_sc[...] = jnp.zeros(s_sc.shape, F32)

    tri = tri_ref[...]
    h64 = h64_ref[...]
    strict = strict_ref[...] != 0
    causal = causal_ref[...] != 0
    nb = r_ref.shape[0]
    c = CHUNK
    where = [(b, slice(j * c, (j + 1) * c)) for j in range(n_chunks) for b in range(nb)]
    load = lambda ref: [ref[b, rows, :].astype(F32) for b, rows in where]
    p = _rwkv_pre(load(r_ref), load(lw_ref), load(k_ref), load(v_ref), load(kk_ref), load(b_ref),
                  tri, h64, strict, causal)
    s = [s_sc[b] for b in range(nb)]
    for j in range(n_chunks):
        items = list(range(j * nb, (j + 1) * nb))
        y, s = _rwkv_post(p, items, s, h64)
        for b in range(nb):
            y_ref[b, j * c:(j + 1) * c, :] = y[b]
    for b in range(nb):
        s_sc[b] = s[b]


def _rwkv(r, lw, k2, v, kk, bvec, masks, *, batch, seq, tc):
    t = r.shape[0]
    const = lambda s: (0, 0)
    spec = pl.BlockSpec((batch, tc, RWKV_WIDTH), lambda s: (0, s, 0))
    seqs = [a.reshape(batch, seq, RWKV_WIDTH) for a in (r, lw, k2, v, kk, bvec)]
    y = pl.pallas_call(
        functools.partial(_rwkv_kernel, n_chunks=tc // CHUNK),
        grid=(seq // tc,),
        in_specs=[spec] * 6 + [pl.BlockSpec(masks["tri"].shape, const),
                               pl.BlockSpec(masks["head64"].shape, const),
                               pl.BlockSpec(masks["strict"].shape, const),
                               pl.BlockSpec(masks["causal"].shape, const)],
        out_specs=spec,
        out_shape=jax.ShapeDtypeStruct((batch, seq, RWKV_WIDTH), F32),
        scratch_shapes=[pltpu.VMEM((batch, RWKV_WIDTH, RWKV_WIDTH), F32)],
        compiler_params=_params("arbitrary"),
        name="rwkv",
    )(*seqs, masks["tri"], masks["head64"], masks["strict"], masks["causal"])
    return y.reshape(t, RWKV_WIDTH)


def _mix_out_kernel(x_ref, oa_ref, ob_ref, gout_ref, yc_ref, bonus_ref, gate_ref, na_ref, nb_ref,
                    lng_ref, lnb_ref, grp_ref, wa_ref, wb_ref, wc_ref, o_ref):
    grp = grp_ref[...]
    ya = _rms(oa_ref[...].astype(F32), na_ref[...])
    ob = ob_ref[...]
    ms = _exact_right(ob * ob, grp) * (1.0 / GLA_DV)
    gout = gout_ref[...]
    yb = ob * lax.rsqrt(ms + NORM_EPS) * nb_ref[...] * (gout * _sigmoid(gout))
    y = yc_ref[...]
    mean = _exact_right(y, grp) * (1.0 / RWKV_N)
    d = y - mean
    var = _exact_right(d * d, grp) * (1.0 / RWKV_N)
    yc = (d * lax.rsqrt(var + RWKV_LN_EPS) * lng_ref[...] + lnb_ref[...] + bonus_ref[...].astype(F32)) * gate_ref[...].astype(F32)
    o_ref[...] = (x_ref[...] + _dot(_bf(ya), wa_ref[...]) + _dot(_bf(yb), wb_ref[...])
                  + _dot(_bf(yc), wc_ref[...]))


def _mix_out(x, oa, ob, pb, yc, bonus, gate, na, nb, lng, lnb, grp, wa, wb, wc, *, tm):
    t, d = x.shape
    const = lambda i: (0, 0)
    row = lambda w: pl.BlockSpec((tm, w), lambda i: (i, 0))
    small = [na, nb, lng, lnb, grp, wa, wb, wc]
    return pl.pallas_call(
        _mix_out_kernel,
        grid=(t // tm,),
        in_specs=[row(d), row(MLA_WIDTH), row(GLA_WIDTH),
                  pl.BlockSpec((tm, GLA_WIDTH), lambda i: (i, 2)),
                  row(RWKV_WIDTH), row(RWKV_WIDTH), row(RWKV_WIDTH)]
                 + [pl.BlockSpec(a.shape, const) for a in small],
        out_specs=row(d),
        out_shape=jax.ShapeDtypeStruct((t, d), F32),
        compiler_params=_params("parallel"),
        name="mix_out",
    )(x, oa, ob, pb, yc, bonus, gate, *small)


def _shift_rows(y, carry):
    r = pltpu.roll(y, 1, 0)
    sub = lax.broadcasted_iota(jnp.int32, carry.shape, 0)
    head = jnp.where(sub == 0, carry[SUBLANES - 1:SUBLANES, :], r[:SUBLANES])
    return jnp.concatenate([head, r[SUBLANES:]], axis=0)


def _causal_conv3(u, prev, cw, cb):
    c0, c1, c2 = cw[0:1, :], cw[1:2, :], cw[2:3, :]
    z_prev = c0 * prev
    w_prev = c1 * prev + pltpu.roll(z_prev, 1, 0)
    w = c1 * u + _shift_rows(c0 * u, z_prev)
    return c2 * u + cb + _shift_rows(w, w_prev)


def _ffn_kernel(x_ref, g_ref, wu_ref, cw_ref, cb_ref, wd_ref, fg_ref, o_ref, cg_sc, cv_sc,
                *, d_ff, tf, final_norm):
    x = x_ref[...]
    h = _bf(_rms(x, g_ref[...]))
    ts = h.shape[0]
    first = pl.program_id(1) == 0
    n_f = d_ff // tf

    def up(c):
        return (_dot(h, wu_ref[:, c * tf:(c + 1) * tf]),
                _dot(h, wu_ref[:, d_ff + c * tf:d_ff + (c + 1) * tf]))

    acc = None
    nxt = up(0)
    for c in range(n_f):
        ug, uv = nxt
        if c + 1 < n_f:
            nxt = up(c + 1)
        gcol = slice(c * tf, (c + 1) * tf)
        vcol = slice(d_ff + c * tf, d_ff + (c + 1) * tf)
        gate = _causal_conv3(ug, jnp.where(first, 0.0, cg_sc[c]), cw_ref[:, gcol], cb_ref[:, gcol])
        val = _causal_conv3(uv, jnp.where(first, 0.0, cv_sc[c]), cw_ref[:, vcol], cb_ref[:, vcol])
        cg_sc[c] = ug[ts - SUBLANES:, :]
        cv_sc[c] = uv[ts - SUBLANES:, :]
        part = _dot(_bf(gate * _sigmoid(gate) * val), wd_ref[gcol, :])
        acc = part if acc is None else acc + part
    y = x + acc
    if final_norm:
        y = _rms(y, fg_ref[...])
    o_ref[...] = y


def _ffn(x, g, w_up, conv_w, conv_b, w_down, final_g, *, batch, seq, ts, tf, final_norm):
    t, d = x.shape
    d_ff = w_down.shape[0]
    nsb = seq // ts
    row = lambda b, s: (b * nsb + s, 0)
    const = lambda b, s: (0, 0)
    resident = lambda a: pl.BlockSpec(a.shape, const, pipeline_mode=pl.Buffered(1))
    return pl.pallas_call(
        functools.partial(_ffn_kernel, d_ff=d_ff, tf=tf, final_norm=final_norm),
        grid=(batch, nsb),
        in_specs=[pl.BlockSpec((ts, d), row),
                  pl.BlockSpec((1, d), const),
                  resident(w_up), resident(conv_w), resident(conv_b), resident(w_down),
                  pl.BlockSpec((1, d), const)],
        out_specs=pl.BlockSpec((ts, d), row),
        out_shape=jax.ShapeDtypeStruct((t, d), F32),
        scratch_shapes=[pltpu.VMEM((d_ff // tf, SUBLANES, tf), F32),
                        pltpu.VMEM((d_ff // tf, SUBLANES, tf), F32)],
        compiler_params=_params("parallel", "arbitrary"),
        name="ffn",
    )(x, g, w_up, conv_w, conv_b, w_down, final_g)


def _rot_cols(w):
    half = w.shape[-1] // 2
    return jnp.concatenate([-w[..., half:], w[..., :half]], axis=-1)


def _regroup_w_in(w_in):
    d = w_in.shape[0]
    a0 = 0
    b0 = MLA_COLS
    c0 = MLA_COLS + GLA_COLS
    k_pe = w_in[:, MLA_Q_LORA + MLA_KV_LORA:MLA_COLS]
    gq = w_in[:, b0:b0 + GLA_QK_WIDTH]
    gk = w_in[:, b0 + GLA_QK_WIDTH:b0 + 2 * GLA_QK_WIDTH]
    gv = w_in[:, b0 + 2 * GLA_QK_WIDTH:b0 + 2 * GLA_QK_WIDTH + GLA_WIDTH]
    glo = w_in[:, b0 + 2 * GLA_QK_WIDTH + GLA_WIDTH:b0 + 2 * GLA_QK_WIDTH + GLA_WIDTH + GLA_GATE_RANK]
    gout = w_in[:, b0 + 2 * GLA_QK_WIDTH + GLA_WIDTH + GLA_GATE_RANK:c0]
    pad = jnp.zeros((d, LANES - GLA_GATE_RANK), w_in.dtype)
    cols = [w_in[:, a0:MLA_COLS], _rot_cols(k_pe), gq, gk, gv, gout, glo, pad, w_in[:, c0:]]
    return _bf(jnp.concatenate(cols, axis=1))


def _regroup_w_uq(w_uq):
    r = w_uq.shape[0]
    w = w_uq.reshape(r, MLA_HEADS, MLA_QK)
    nope = w[:, :, :MLA_NOPE]
    pe = w[:, :, MLA_NOPE:]
    return _bf(jnp.concatenate([nope, pe, _rot_cols(pe)], axis=-1).reshape(r, MLA_HEADS * QK_HEAD_W))


def _regroup_w_ukv(w_ukv):
    r = w_ukv.shape[0]
    w = w_ukv.reshape(r, MLA_HEADS, MLA_NOPE + MLA_V)
    k_nope = w[:, :, :MLA_NOPE].reshape(r, MLA_HEADS * MLA_NOPE)
    v = w[:, :, MLA_NOPE:].reshape(r, MLA_HEADS * MLA_V)
    return _bf(jnp.concatenate([k_nope, v], axis=1))


def _rope_table(seq):
    inv = 1.0 / (ROPE_THETA ** (jnp.arange(0, MLA_ROPE, 2, dtype=F32) / MLA_ROPE))
    ang = jnp.arange(seq, dtype=F32)[:, None] * inv[None, :]
    cos, sin = jnp.cos(ang), jnp.sin(ang)
    return jnp.concatenate([cos, cos, sin, sin], axis=1)


def _row(v):
    return v.reshape(1, -1).astype(F32)


def _pad_rows(w, top, total):
    return jnp.concatenate([jnp.zeros((top, w.shape[1]), w.dtype), w,
                            jnp.zeros((total - top - w.shape[0], w.shape[1]), w.dtype)], axis=0)


def _tiles(seq):
    return {
        "tm": min(512, seq),
        "tk": min(512, seq // 2),
        "tc": min(256, seq),
        "tr": min(128, seq),
        "ts": min(512, seq),
        "tf": 256,
    }


def kernel(x, ln1_g, w_in, mla_q_norm_g, mla_w_uq, mla_kv_norm_g, mla_w_ukv, mla_out_norm_g, gla_w_gk, gla_b_gk, gla_norm_g, rwkv_mu, rwkv_w0, rwkv_w2, rwkv_a0, rwkv_a2, rwkv_g2, rwkv_k_k, rwkv_k_a, rwkv_r_k, rwkv_ln_g, rwkv_ln_b, w_out, ln2_g, ffn_w_up, ffn_conv_w, ffn_conv_b, ffn_w_down, final_g):
    batch, seq, d_model = x.shape
    depth = w_in.shape[0]
    tl = _tiles(seq)
    masks = _mask_inputs()
    grp = masks["group64"]
    rope = _rope_table(seq)
    xt = x.reshape(batch * seq, d_model)
    for l in range(depth):
        pa, pb, pc = _in_proj(xt, _row(ln1_g[l]), _regroup_w_in(w_in[l]), tm=tl["tm"])
        q, k, v = _mla_prep(pa, _row(mla_q_norm_g[l]), _row(mla_kv_norm_g[l]), _regroup_w_uq(mla_w_uq[l]),
                            _regroup_w_ukv(mla_w_ukv[l]), rope, tm=tl["tm"], seq=seq)
        o_mla = _flash(q, k, v, batch=batch, seq=seq, tk=tl["tk"])
        wgk = _bf(_pad_rows(gla_w_gk[l], 0, LANES))
        o_gla = _gla(pb, wgk, _row(gla_b_gk[l]), masks, batch=batch, seq=seq, tc=tl["tc"])
        w2p = _bf(_pad_rows(rwkv_w2[l], 0, LANES))
        a2p = _bf(_pad_rows(rwkv_a2[l], RWKV_DECAY_RANK, LANES))
        r, lw, k2, vv, kk, bvec, gate, bonus = _rwkv_prep(
            pc, _row(rwkv_mu[l]), _row(rwkv_w0[l]), w2p, _row(rwkv_a0[l]), a2p, _bf(rwkv_g2[l]),
            _row(rwkv_k_k[l]), _row(rwkv_k_a[l]), _row(rwkv_r_k[l]), grp, tm=tl["tm"], seq=seq)
        y_rwkv = _rwkv(r, lw, k2, vv, kk, bvec, masks, batch=batch, seq=seq, tc=tl["tr"])
        wo = _bf(w_out[l])
        xt = _mix_out(xt, o_mla, o_gla, pb, y_rwkv, bonus, gate, _row(mla_out_norm_g[l]),
                      _row(jnp.tile(gla_norm_g[l], GLA_HEADS)), _row(rwkv_ln_g[l]), _row(rwkv_ln_b[l]), grp,
                      wo[:MLA_WIDTH], wo[MLA_WIDTH:MLA_WIDTH + GLA_WIDTH], wo[MLA_WIDTH + GLA_WIDTH:],
                      tm=tl["tm"])
        xt = _ffn(xt, _row(ln2_g[l]), _bf(ffn_w_up[l]), ffn_conv_w[l].astype(F32), _row(ffn_conv_b[l]),
                  _bf(ffn_w_down[l]), _row(final_g), batch=batch, seq=seq, ts=tl["ts"], tf=tl["tf"],
                  final_norm=(l == depth - 1))
    return xt.reshape(batch, seq, d_model)
```

```python
import functools
import math

import numpy as np
import jax
import jax.numpy as jnp
from jax import lax
from jax.experimental import pallas as pl
from jax.experimental.pallas import tpu as pltpu

F32 = jnp.float32
BF16 = jnp.bfloat16

MLA_HEADS = 4
MLA_NOPE = 128
MLA_ROPE = 64
MLA_V = 128
MLA_QK = MLA_NOPE + MLA_ROPE
MLA_Q_LORA = 384
MLA_KV_LORA = 256
MLA_WIDTH = MLA_HEADS * MLA_V
ROPE_THETA = 10000.0
GLA_HEADS = 4
GLA_DK = 32
GLA_DV = 64
GLA_QK_WIDTH = GLA_HEADS * GLA_DK
GLA_WIDTH = GLA_HEADS * GLA_DV
GLA_GATE_RANK = 16
GLA_GATE_NORM = 16.0
RWKV_HEADS = 4
RWKV_N = 64
RWKV_WIDTH = RWKV_HEADS * RWKV_N
RWKV_DECAY_RANK = 64
RWKV_A_RANK = 64
RWKV_GATE_RANK = 128
RWKV_LN_EPS = 64e-5
MLA_COLS = MLA_Q_LORA + MLA_KV_LORA + MLA_ROPE
GLA_COLS = 2 * GLA_QK_WIDTH + GLA_WIDTH + GLA_GATE_RANK + GLA_WIDTH
RWKV_COLS = 3 * RWKV_WIDTH + RWKV_DECAY_RANK + RWKV_A_RANK + RWKV_GATE_RANK
NORM_EPS = 1e-6
CONV_WIDTH = 3

LANES = 128
SUBLANES = 8
VMEM_LIMIT = 56 * 1024 * 1024

PA_W = MLA_Q_LORA + MLA_KV_LORA + 2 * MLA_ROPE
PB_W = 2 * GLA_QK_WIDTH + 2 * GLA_WIDTH + LANES
PC_W = RWKV_COLS
QK_HEAD_W = 2 * LANES

LOG2E = math.log2(math.e)

CHUNK = 64


def _dot(a, b):
    return lax.dot_general(a, b, (((1,), (0,)), ((), ())), preferred_element_type=F32)


def _dot_nt(a, b):
    return lax.dot_general(a, b, (((1,), (1,)), ((), ())), preferred_element_type=F32)


def _dot_tn(a, b):
    return lax.dot_general(a, b, (((0,), (0,)), ((), ())), preferred_element_type=F32)


def _bf(x):
    return x.astype(BF16)


def _split_terms(x, n):
    terms = []
    rem = x
    for _ in range(n):
        t = rem.astype(BF16)
        terms.append(t)
        rem = rem - t.astype(F32)
    return terms


def _exact_left(m, x, n=3):
    out = None
    for t in _split_terms(x, n):
        y = _dot(m, t)
        out = y if out is None else out + y
    return out


def _exact_right(x, m, n=2):
    out = None
    for t in _split_terms(x, n):
        y = _dot(t, m)
        out = y if out is None else out + y
    return out


def _rms(x, g, eps=NORM_EPS):
    return x * lax.rsqrt(jnp.mean(x * x, axis=-1, keepdims=True) + eps) * g


def _sigmoid(x):
    return 1.0 / (1.0 + jnp.exp(-x))


def _softplus(x):
    return jnp.maximum(x, 0.0) + jnp.log(1.0 + jnp.exp(-jnp.abs(x)))


def _params(*sem):
    return pltpu.CompilerParams(dimension_semantics=sem, vmem_limit_bytes=VMEM_LIMIT)


def _proj_kernel(x_ref, g_ref, w_ref, gq_ref, gkv_ref, wq_ref, wkv_ref, rope_ref, mu_ref, w0_ref, w2_ref,
                 a0_ref, a2_ref, g2_ref, kk_ref, ka_ref, rk_ref, grp_ref,
                 q_out, k_out, v_out, pb_out, r_out, lw_out, kr_out, vr_out, kkr_out, b_out, gate_out, bonus_out,
                 carry_sc, *, blocks_per_seq, n_step):
    h = _bf(_rms(x_ref[...], g_ref[...]))

    def group(base, width):
        cols = [_dot(h, w_ref[:, base + c:base + min(c + n_step, width)]) for c in range(0, width, n_step)]
        return cols[0] if len(cols) == 1 else jnp.concatenate(cols, axis=1)

    _mla_prep_body(group(0, PA_W), gq_ref, gkv_ref, wq_ref, wkv_ref, rope_ref, q_out, k_out, v_out)
    pb_out[...] = _bf(group(PA_W, PB_W))
    pc = group(PA_W + PB_W, PC_W)
    first = (pl.program_id(0) % blocks_per_seq) == 0
    last_prev = jnp.where(first, 0.0, carry_sc[SUBLANES - 1:SUBLANES, :])
    carry_sc[...] = pc[pc.shape[0] - SUBLANES:, :]
    _rwkv_prep_body(pc, last_prev, mu_ref, w0_ref, w2_ref, a0_ref, a2_ref, g2_ref, kk_ref, ka_ref, rk_ref, grp_ref,
                    r_out, lw_out, kr_out, vr_out, kkr_out, b_out, gate_out, bonus_out)


def _proj(x, g, w, mla, rwkv, rope, grp, *, tm, seq):
    t, d = x.shape
    const = lambda i: (0, 0)
    resident = lambda a: pl.BlockSpec(a.shape, const, pipeline_mode=pl.Buffered(1))
    row = lambda width: pl.BlockSpec((tm, width), lambda i: (i, 0))
    sds = lambda width, dt: jax.ShapeDtypeStruct((t, width), dt)
    qk_w = MLA_HEADS * QK_HEAD_W
    n_seq_blocks = seq // tm
    return pl.pallas_call(
        functools.partial(_proj_kernel, blocks_per_seq=n_seq_blocks, n_step=512),
        grid=(t // tm,),
        in_specs=[row(d), pl.BlockSpec((1, d), const), resident(w)]
                 + [resident(a) for a in mla]
                 + [pl.BlockSpec((tm, LANES), lambda i: (i % n_seq_blocks, 0))]
                 + [resident(a) for a in rwkv] + [resident(grp)],
        out_specs=[row(qk_w), row(qk_w), row(2 * MLA_WIDTH), row(PB_W)] + [row(RWKV_WIDTH)] * 8,
        out_shape=[sds(qk_w, BF16), sds(qk_w, BF16), sds(2 * MLA_WIDTH, BF16), sds(PB_W, BF16),
                   sds(RWKV_WIDTH, BF16), sds(RWKV_WIDTH, F32)] + [sds(RWKV_WIDTH, BF16)] * 6,
        scratch_shapes=[pltpu.VMEM((SUBLANES, PC_W), F32)],
        compiler_params=_params("arbitrary"),
        name="proj",
    )(x, g, w, *mla, rope, *rwkv, grp)


def _mla_prep_body(pa, gq_ref, gkv_ref, wq_ref, wkv_ref, rope_ref, q_ref, k_ref, v_ref):
    rope = rope_ref[...]
    cq = _bf(_rms(pa[:, :MLA_Q_LORA], gq_ref[...]))
    q = _dot(cq, wq_ref[...]) * (MLA_QK ** -0.5 * LOG2E)
    for h in range(MLA_HEADS):
        b = h * QK_HEAD_W
        q_ref[:, b:b + LANES] = _bf(q[:, b:b + LANES])
        q_ref[:, b + LANES:b + QK_HEAD_W] = _bf(q[:, b + LANES:b + QK_HEAD_W] * rope)
    ckv = _bf(_rms(pa[:, MLA_Q_LORA:MLA_Q_LORA + MLA_KV_LORA], gkv_ref[...]))
    kv = _dot(ckv, wkv_ref[...])
    kr = pa[:, MLA_Q_LORA + MLA_KV_LORA:] * rope
    kr = _bf(kr + pltpu.roll(kr, MLA_ROPE, 1))
    for h in range(MLA_HEADS):
        b = h * QK_HEAD_W
        k_ref[:, b:b + LANES] = _bf(kv[:, h * MLA_NOPE:(h + 1) * MLA_NOPE])
        k_ref[:, b + LANES:b + QK_HEAD_W] = kr
    ones = jnp.ones((pa.shape[0], MLA_V), BF16)
    for h in range(MLA_HEADS):
        b = h * 2 * MLA_V
        v_ref[:, b:b + MLA_V] = _bf(kv[:, (MLA_HEADS + h) * MLA_NOPE:(MLA_HEADS + h + 1) * MLA_NOPE])
        v_ref[:, b + MLA_V:b + 2 * MLA_V] = ones


def _flash_kernel(q_ref, k_ref, v_ref, o_ref, m_sc, acc_sc, s_sc, *, tk):
    qi = pl.program_id(2)
    m_sc[...] = jnp.full(m_sc.shape, -jnp.inf, F32)
    acc_sc[...] = jnp.zeros(acc_sc.shape, F32)

    def kv_rows(j):
        return pl.ds(pl.multiple_of(j * tk, tk), tk)

    def scores(sub, j):
        return _dot_nt(q_ref[sub * tk:(sub + 1) * tk, :], k_ref[kv_rows(j), :])

    def accumulate(sub, j, s, masked):
        rows = slice(sub * tk, (sub + 1) * tk)
        if masked:
            row = lax.broadcasted_iota(jnp.int32, s.shape, 0)
            col = lax.broadcasted_iota(jnp.int32, s.shape, 1)
            s = jnp.where(row >= col, s, -jnp.inf)
        m_prev = m_sc[rows, :]
        m_next = jnp.maximum(m_prev, jnp.max(s, axis=1, keepdims=True))
        alpha = jnp.exp2(m_prev - m_next)
        p = jnp.exp2(s - jnp.concatenate([m_next] * (tk // LANES), axis=1))
        acc_sc[rows, :] = (acc_sc[rows, :] * jnp.concatenate([alpha, alpha], axis=1)
                           + _dot(_bf(p), v_ref[kv_rows(j), :]))
        m_sc[rows, :] = m_next

    for sub in range(2):
        s_sc[0, sub] = scores(sub, 0)

    def body(jj, carry):
        j = 2 * jj
        for slot in range(2):
            for sub in range(2):
                s = s_sc[slot, sub]
                s_sc[1 - slot, sub] = scores(sub, j + slot + 1)
                accumulate(sub, j + slot, s, False)
        return carry

    lax.fori_loop(0, qi, body, 0)
    j = 2 * qi
    s_last = scores(1, j + 1)
    accumulate(0, j, s_sc[0, 0], True)
    accumulate(1, j, s_sc[0, 1], False)
    accumulate(1, j + 1, s_last, True)
    acc = acc_sc[...]
    o_ref[...] = _bf(acc[:, :MLA_V] / acc[:, MLA_V:])


def _flash(q, k, v1, *, batch, seq, tk):
    tq = 2 * tk
    nq = seq // tq
    return pl.pallas_call(
        functools.partial(_flash_kernel, tk=tk),
        grid=(batch, MLA_HEADS, nq),
        in_specs=[pl.BlockSpec((tq, QK_HEAD_W), lambda b, h, i: (b * nq + i, h)),
                  pl.BlockSpec((seq, QK_HEAD_W), lambda b, h, i: (b, h)),
                  pl.BlockSpec((seq, 2 * MLA_V), lambda b, h, i: (b, h))],
        out_specs=pl.BlockSpec((tq, MLA_V), lambda b, h, i: (b * nq + i, h)),
        out_shape=jax.ShapeDtypeStruct((batch * seq, MLA_WIDTH), BF16),
        scratch_shapes=[pltpu.VMEM((tq, LANES), F32),
                        pltpu.VMEM((tq, 2 * MLA_V), F32),
                        pltpu.VMEM((2, 2, tk, tk), F32)],
        compiler_params=_params("parallel", "parallel", "arbitrary"),
        name="flash",
    )(q, k, v1)


def _np_masks():
    c = CHUNK
    r4 = np.arange(4 * c)
    tri = (np.arange(c)[:, None] >= np.arange(c)[None, :])
    m = {
        "tri": tri,
        "head64": (r4[:, None] // c) == (np.arange(RWKV_WIDTH)[None, :] // RWKV_N),
        "head32": (r4[:, None] // c) == (np.arange(GLA_QK_WIDTH)[None, :] // GLA_DK),
        "strict": (r4[:, None] % c) > (r4[None, :] % c),
        "causal": np.arange(c)[:, None] >= (r4[None, :] % c),
    }
    return m


def _mask_inputs():
    m = _np_masks()
    return {
        "tri": jnp.asarray(m["tri"], BF16),
        "head64": jnp.asarray(m["head64"], F32),
        "head32": jnp.asarray(m["head32"], F32),
        "strict": jnp.asarray(m["strict"], F32),
        "causal": jnp.asarray(m["causal"], F32),
        "group64": jnp.asarray(m["head64"], BF16),
    }


def _stack(x, head_mask):
    return jnp.concatenate([x] * 4, axis=0) * head_mask


def _unstack(x):
    c = x.shape[0] // 4
    return x[0:c] + x[c:2 * c] + x[2 * c:3 * c] + x[3 * c:4 * c]


def _gla_kernel(qk_ref, v_ref, glo_ref, wgk_ref, bgk_ref, tri_ref, h32_ref, h64_ref, causal_ref,
                o_ref, st_sc, *, n_chunks):
    @pl.when(pl.program_id(0) == 0)
    def _():
        st_sc[...] = jnp.zeros(st_sc.shape, F32)

    c = CHUNK
    tri = tri_ref[...]
    h32 = h32_ref[...]
    h64 = h64_ref[...]
    causal = causal_ref[...] != 0
    nb = qk_ref.shape[0]
    items = [(b, slice(j * c, (j + 1) * c)) for j in range(n_chunks) for b in range(nb)]
    pre = [_dot(glo_ref[b], wgk_ref[...]) + bgk_ref[...] for b in range(nb)]
    gk = [-_softplus(-pre[b][rows, :]) / GLA_GATE_NORM for b, rows in items]
    q = [qk_ref[b, rows, :GLA_QK_WIDTH].astype(F32) for b, rows in items]
    k = [qk_ref[b, rows, GLA_QK_WIDTH:].astype(F32) for b, rows in items]
    v = [v_ref[b, rows, :] for b, rows in items]
    bc = _each(lambda x: _exact_left(tri, x), gk)
    b_last = _each(lambda x: x[c - 1:c, :], bc)
    q_t = _each(lambda q_, b_: q_ * (GLA_DK ** -0.5) * jnp.exp(b_), q, bc)
    k_t = _each(lambda k_, b_: k_ * jnp.exp(-b_), k, bc)
    k_end = _each(lambda k_, bl, b_: k_ * jnp.exp(bl - b_), k, b_last, bc)
    a = _each(lambda q_, k_: jnp.where(causal, _dot_nt(_bf(q_), _bf(_stack(k_, h32))), 0.0), q_t, k_t)
    o_intra = _each(lambda a_, v_: _dot(_bf(a_), _bf(_stack(v_, h64))), a, v)
    upd = _each(lambda v_, ke: _dot_tn(_bf(v_), _bf(ke)) * h32, v, k_end)
    decay = _each(jnp.exp, b_last)
    st = [st_sc[b] for b in range(nb)]
    for i, (b, rows) in enumerate(items):
        o_ref[b, rows, :] = o_intra[i] + _dot_nt(_bf(q_t[i]), _bf(st[b]))
        st[b] = st[b] * decay[i] + upd[i]
    for b in range(nb):
        st_sc[b] = st[b]


def _gla(pb, wgk, bgk, masks, *, batch, seq, tc):
    t = pb.shape[0]
    const = lambda s: (0, 0)
    pb3 = pb.reshape(batch, seq, pb.shape[1])
    o = pl.pallas_call(
        functools.partial(_gla_kernel, n_chunks=tc // CHUNK),
        grid=(seq // tc,),
        in_specs=[pl.BlockSpec((batch, tc, 2 * GLA_QK_WIDTH), lambda s: (0, s, 0)),
                  pl.BlockSpec((batch, tc, GLA_WIDTH), lambda s: (0, s, 1)),
                  pl.BlockSpec((batch, tc, LANES), lambda s: (0, s, 6)),
                  pl.BlockSpec(wgk.shape, const),
                  pl.BlockSpec(bgk.shape, const),
                  pl.BlockSpec(masks["tri"].shape, const),
                  pl.BlockSpec(masks["head32"].shape, const),
                  pl.BlockSpec(masks["head64"].shape, const),
                  pl.BlockSpec(masks["causal"].shape, const)],
        out_specs=pl.BlockSpec((batch, tc, GLA_WIDTH), lambda s: (0, s, 0)),
        out_shape=jax.ShapeDtypeStruct((batch, seq, GLA_WIDTH), F32),
        scratch_shapes=[pltpu.VMEM((batch, GLA_WIDTH, GLA_QK_WIDTH), F32)],
        compiler_params=_params("arbitrary"),
        name="gla",
    )(pb3, pb3, pb3, wgk, bgk, masks["tri"], masks["head32"], masks["head64"], masks["causal"])
    return o.reshape(t, GLA_WIDTH)


def _rwkv_prep_body(xc, last_prev, mu_ref, w0_ref, w2_ref, a0_ref, a2_ref, g2_ref, kk_ref, ka_ref,
                    rk_ref, grp_ref, r_out, lw_out, k_out, v_out, kk_out, b_out, g_out, bonus_out):
    row = lax.broadcasted_iota(jnp.int32, xc.shape, 0)
    prev = jnp.where(row == 0, last_prev, pltpu.roll(xc, 1, 0))
    xm = xc + (prev - xc) * mu_ref[...]
    w3 = RWKV_WIDTH
    r = xm[:, 0:w3]
    k = xm[:, w3:2 * w3]
    v = xm[:, 2 * w3:3 * w3]
    wa = xm[:, 3 * w3:3 * w3 + LANES]
    g_lo = xm[:, 3 * w3 + LANES:]
    w = -_softplus(-(w0_ref[...] + _dot(_bf(jnp.tanh(wa)), w2_ref[...]))) - 0.5
    a = _sigmoid(a0_ref[...] + _dot(_bf(wa), a2_ref[...]))
    g = _dot(_bf(_sigmoid(g_lo)), g2_ref[...])
    grp = grp_ref[...]
    kk = k * kk_ref[...]
    norm = jnp.sqrt(_exact_right(kk * kk, grp))
    kk = kk / jnp.maximum(norm, 1e-12)
    k2 = k * (1.0 + (a - 1.0) * ka_ref[...])
    r_out[...] = _bf(r)
    lw_out[...] = -jnp.exp(w)
    k_out[...] = _bf(k2)
    v_out[...] = _bf(v)
    kk_out[...] = _bf(kk)
    b_out[...] = _bf(kk * a)
    g_out[...] = _bf(g)
    bonus_out[...] = _bf(_exact_right(r * k2 * rk_ref[...], grp) * v)


def _each(fn, *lists):
    return [fn(*xs) for xs in zip(*lists)]


def _rwkv_pre(r, lw, k2, v, kk, bvec, tri, h64, strict, causal):
    c = CHUNK
    g = _each(lambda x: _exact_left(tri, x), lw)
    g_last = _each(lambda x: x[c - 1:c, :], g)
    e_neg = _each(lambda x: jnp.exp(-x), g)
    a_t = _each(lambda kk_, g_, lw_: -kk_ * jnp.exp(g_ - lw_), kk, g, lw)
    r_t = _each(lambda r_, g_: r_ * jnp.exp(g_), r, g)
    b_t = _each(jnp.multiply, bvec, e_neg)
    k_t = _each(jnp.multiply, k2, e_neg)
    e_end = _each(lambda gl, g_: jnp.exp(gl - g_), g_last, g)
    b_end = _each(jnp.multiply, bvec, e_end)
    k_end = _each(jnp.multiply, k2, e_end)

    a_s = _each(lambda x: _bf(_stack(x, h64)), a_t)
    bk_s = _each(lambda b_, k_: _bf(jnp.concatenate([_stack(b_, h64), _stack(k_, h64)], axis=0)), b_t, k_t)
    v_s = _each(lambda x: _bf(_stack(x, h64)), v)
    ab_ak = _each(_dot_nt, a_s, bk_s)
    l1 = _each(lambda x: jnp.where(strict, x[:, :4 * c], 0.0), ab_ak)
    a_ak = _each(lambda x: jnp.where(strict, x[:, 4 * c:], 0.0), ab_ak)
    rb_rk = _each(lambda r_, bk: _dot_nt(_bf(r_), bk), r_t, bk_s)
    a_rb = _each(lambda x: jnp.where(causal, x[:, :4 * c], 0.0), rb_rk)
    a_rk = _each(lambda x: jnp.where(causal, x[:, 4 * c:], 0.0), rb_rk)

    shape = l1[0].shape
    eye = (lax.broadcasted_iota(jnp.int32, shape, 0) == lax.broadcasted_iota(jnp.int32, shape, 1)).astype(F32)
    t_inv = _each(lambda x: eye + x, l1)
    lp = l1
    for _ in range(5):
        lp = _each(lambda x: _dot(_bf(x), _bf(x)), lp)
        t_inv = _each(lambda t, p: t + _dot(_bf(t), _bf(p)), t_inv, lp)
    t_b = _each(_bf, t_inv)
    av = _each(lambda a, vs: _bf(_dot(_bf(a), vs)), a_ak, v_s)
    w_a = _each(lambda t, a: _unstack(_dot(t, a)), t_b, a_s)
    u_v = _each(lambda t, x: _unstack(_dot(t, x)), t_b, av)

    y_v = _each(lambda ark, vs: _dot(_bf(ark), vs), a_rk, v_s)
    bk_end = _each(lambda b_, k_: _bf(jnp.concatenate([b_, k_], axis=0)), b_end, k_end)
    decay = _each(jnp.exp, g_last)
    return {"w_a": _each(_bf, w_a), "u_v": u_v, "r_t": _each(_bf, r_t), "a_rb": _each(_bf, a_rb), "y_v": y_v,
            "v": v, "bk_end": bk_end, "decay": decay}


def _rwkv_post(p, items, s, h64):
    pick = lambda name: [p[name][i] for i in items]
    s_b = _each(_bf, s)
    u = _each(lambda w, sb, uv_: _dot_nt(w, sb) + uv_, pick("w_a"), s_b, pick("u_v"))
    y = _each(lambda r_, sb, arb, u_, yv: _dot_nt(r_, sb) + _dot(arb, _bf(_stack(u_, h64))) + yv,
              pick("r_t"), s_b, pick("a_rb"), u, pick("y_v"))
    uv = _each(lambda u_, v_: _bf(jnp.concatenate([u_, v_], axis=0)), u, pick("v"))
    s_new = _each(lambda s_, d, x, z: s_ * d + _dot_tn(x, z) * h64, s, pick("decay"), uv, pick("bk_end"))
    return y, s_new


def _rwkv_kernel(r_ref, lw_ref, k_ref, v_ref, kk_ref, b_ref, tri_ref, h64_ref, strict_ref, causal_ref,
                 y_ref, s_sc, *, n_chunks):
    @pl.when(pl.program_id(0) == 0)
    def _():
        s_sc[...] = jnp.zeros(s_sc.shape, F32)

    tri = tri_ref[...]
    h64 = h64_ref[...]
    strict = strict_ref[...] != 0
    causal = causal_ref[...] != 0
    nb = r_ref.shape[0]
    c = CHUNK
    where = [(b, slice(j * c, (j + 1) * c)) for j in range(n_chunks) for b in range(nb)]
    load = lambda ref: [ref[b, rows, :].astype(F32) for b, rows in where]
    p = _rwkv_pre(load(r_ref), load(lw_ref), load(k_ref), load(v_ref), load(kk_ref), load(b_ref),
                  tri, h64, strict, causal)
    s = [s_sc[b] for b in range(nb)]
    for j in range(n_chunks):
        items = list(range(j * nb, (j + 1) * nb))
        y, s = _rwkv_post(p, items, s, h64)
        for b in range(nb):
            y_ref[b, j * c:(j + 1) * c, :] = y[b]
    for b in range(nb):
        s_sc[b] = s[b]


def _rwkv(r, lw, k2, v, kk, bvec, masks, *, batch, seq, tc):
    t = r.shape[0]
    const = lambda s: (0, 0)
    spec = pl.BlockSpec((batch, tc, RWKV_WIDTH), lambda s: (0, s, 0))
    seqs = [a.reshape(batch, seq, RWKV_WIDTH) for a in (r, lw, k2, v, kk, bvec)]
    y = pl.pallas_call(
        functools.partial(_rwkv_kernel, n_chunks=tc // CHUNK),
        grid=(seq // tc,),
        in_specs=[spec] * 6 + [pl.BlockSpec(masks["tri"].shape, const),
                               pl.BlockSpec(masks["head64"].shape, const),
                               pl.BlockSpec(masks["strict"].shape, const),
                               pl.BlockSpec(masks["causal"].shape, const)],
        out_specs=spec,
        out_shape=jax.ShapeDtypeStruct((batch, seq, RWKV_WIDTH), F32),
        scratch_shapes=[pltpu.VMEM((batch, RWKV_WIDTH, RWKV_WIDTH), F32)],
        compiler_params=_params("arbitrary"),
        name="rwkv",
    )(*seqs, masks["tri"], masks["head64"], masks["strict"], masks["causal"])
    return y.reshape(t, RWKV_WIDTH)


def _mix_out_body(x, oa_ref, ob_ref, gout_ref, yc_ref, bonus_ref, gate_ref, na_ref, nb_ref,
                  lng_ref, lnb_ref, grp_ref, wa_ref, wb_ref, wc_ref):
    grp = grp_ref[...]
    ya = _rms(oa_ref[...].astype(F32), na_ref[...])
    ob = ob_ref[...]
    ms = _exact_right(ob * ob, grp) * (1.0 / GLA_DV)
    gout = gout_ref[...].astype(F32)
    yb = ob * lax.rsqrt(ms + NORM_EPS) * nb_ref[...] * (gout * _sigmoid(gout))
    y = yc_ref[...]
    mean = _exact_right(y, grp) * (1.0 / RWKV_N)
    d = y - mean
    var = _exact_right(d * d, grp) * (1.0 / RWKV_N)
    yc = (d * lax.rsqrt(var + RWKV_LN_EPS) * lng_ref[...] + lnb_ref[...] + bonus_ref[...].astype(F32)) * gate_ref[...].astype(F32)
    return x + _dot(_bf(ya), wa_ref[...]) + _dot(_bf(yb), wb_ref[...]) + _dot(_bf(yc), wc_ref[...])


def _shift_rows(y, carry):
    r = pltpu.roll(y, 1, 0)
    sub = lax.broadcasted_iota(jnp.int32, carry.shape, 0)
    head = jnp.where(sub == 0, carry[SUBLANES - 1:SUBLANES, :], r[:SUBLANES])
    return jnp.concatenate([head, r[SUBLANES:]], axis=0)


def _causal_conv3(u, prev, cw, cb):
    c0, c1, c2 = cw[0:1, :], cw[1:2, :], cw[2:3, :]
    z_prev = c0 * prev
    w_prev = c1 * prev + pltpu.roll(z_prev, 1, 0)
    w = c1 * u + _shift_rows(c0 * u, z_prev)
    return c2 * u + cb + _shift_rows(w, w_prev)


def _mix_ffn_kernel(x_ref, oa_ref, ob_ref, gout_ref, yc_ref, bonus_ref, gate_ref, na_ref, nb_ref, lng_ref, lnb_ref,
                    grp_ref, wa_ref, wb_ref, wc_ref, g_ref, wu_ref, cw_ref, cb_ref, wd_ref, fg_ref,
                    o_ref, cg_sc, cv_sc, *, d_ff, tf, final_norm):
    x = _mix_out_body(x_ref[...], oa_ref, ob_ref, gout_ref, yc_ref, bonus_ref, gate_ref, na_ref, nb_ref,
                      lng_ref, lnb_ref, grp_ref, wa_ref, wb_ref, wc_ref)
    h = _bf(_rms(x, g_ref[...]))
    ts = h.shape[0]
    first = pl.program_id(1) == 0
    n_f = d_ff // tf

    def up(c):
        return (_dot(h, wu_ref[:, c * tf:(c + 1) * tf]),
                _dot(h, wu_ref[:, d_ff + c * tf:d_ff + (c + 1) * tf]))

    acc = None
    nxt = up(0)
    for c in range(n_f):
        ug, uv = nxt
        if c + 1 < n_f:
            nxt = up(c + 1)
        gcol = slice(c * tf, (c + 1) * tf)
        vcol = slice(d_ff + c * tf, d_ff + (c + 1) * tf)
        gate = _causal_conv3(ug, jnp.where(first, 0.0, cg_sc[c]), cw_ref[:, gcol], cb_ref[:, gcol])
        val = _causal_conv3(uv, jnp.where(first, 0.0, cv_sc[c]), cw_ref[:, vcol], cb_ref[:, vcol])
        cg_sc[c] = ug[ts - SUBLANES:, :]
        cv_sc[c] = uv[ts - SUBLANES:, :]
        part = _dot(_bf(gate * _sigmoid(gate) * val), wd_ref[gcol, :])
        acc = part if acc is None else acc + part
    y = x + acc
    if final_norm:
        y = _rms(y, fg_ref[...])
    o_ref[...] = y


def _mix_ffn(x, mixers, mix_params, g, w_up, conv_w, conv_b, w_down, final_g, *, batch, seq, ts, tf, final_norm):
    t, d = x.shape
    d_ff = w_down.shape[0]
    nsb = seq // ts
    const = lambda b, s: (0, 0)
    rows = lambda width, col=0: pl.BlockSpec((ts, width), lambda b, s: (b * nsb + s, col))
    resident = lambda a: pl.BlockSpec(a.shape, const, pipeline_mode=pl.Buffered(1))
    return pl.pallas_call(
        functools.partial(_mix_ffn_kernel, d_ff=d_ff, tf=tf, final_norm=final_norm),
        grid=(batch, nsb),
        in_specs=[rows(d), rows(MLA_WIDTH), rows(GLA_WIDTH),
                  rows(GLA_WIDTH, 2),
                  rows(RWKV_WIDTH), rows(RWKV_WIDTH), rows(RWKV_WIDTH)]
                 + [resident(a) for a in mix_params]
                 + [pl.BlockSpec((1, d), const),
                    resident(w_up), resident(conv_w), resident(conv_b), resident(w_down),
                    pl.BlockSpec((1, d), const)],
        out_specs=rows(d),
        out_shape=jax.ShapeDtypeStruct((t, d), F32),
        scratch_shapes=[pltpu.VMEM((d_ff // tf, SUBLANES, tf), F32),
                        pltpu.VMEM((d_ff // tf, SUBLANES, tf), F32)],
        compiler_params=_params("parallel", "arbitrary"),
        name="mix_ffn",
    )(x, *mixers, *mix_params, g, w_up, conv_w, conv_b, w_down, final_g)


def _rot_cols(w):
    half = w.shape[-1] // 2
    return jnp.concatenate([-w[..., half:], w[..., :half]], axis=-1)


def _regroup_w_in(w_in):
    d = w_in.shape[0]
    a0 = 0
    b0 = MLA_COLS
    c0 = MLA_COLS + GLA_COLS
    k_pe = w_in[:, MLA_Q_LORA + MLA_KV_LORA:MLA_COLS]
    gq = w_in[:, b0:b0 + GLA_QK_WIDTH]
    gk = w_in[:, b0 + GLA_QK_WIDTH:b0 + 2 * GLA_QK_WIDTH]
    gv = w_in[:, b0 + 2 * GLA_QK_WIDTH:b0 + 2 * GLA_QK_WIDTH + GLA_WIDTH]
    glo = w_in[:, b0 + 2 * GLA_QK_WIDTH + GLA_WIDTH:b0 + 2 * GLA_QK_WIDTH + GLA_WIDTH + GLA_GATE_RANK]
    gout = w_in[:, b0 + 2 * GLA_QK_WIDTH + GLA_WIDTH + GLA_GATE_RANK:c0]
    pad = jnp.zeros((d, LANES - GLA_GATE_RANK), w_in.dtype)
    cols = [w_in[:, a0:MLA_COLS], _rot_cols(k_pe), gq, gk, gv, gout, glo, pad, w_in[:, c0:]]
    return _bf(jnp.concatenate(cols, axis=1))


def _regroup_w_uq(w_uq):
    r = w_uq.shape[0]
    w = w_uq.reshape(r, MLA_HEADS, MLA_QK)
    nope = w[:, :, :MLA_NOPE]
    pe = w[:, :, MLA_NOPE:]
    return _bf(jnp.concatenate([nope, pe, _rot_cols(pe)], axis=-1).reshape(r, MLA_HEADS * QK_HEAD_W))


def _regroup_w_ukv(w_ukv):
    r = w_ukv.shape[0]
    w = w_ukv.reshape(r, MLA_HEADS, MLA_NOPE + MLA_V)
    k_nope = w[:, :, :MLA_NOPE].reshape(r, MLA_HEADS * MLA_NOPE)
    v = w[:, :, MLA_NOPE:].reshape(r, MLA_HEADS * MLA_V)
    return _bf(jnp.concatenate([k_nope, v], axis=1))


def _rope_table(seq):
    inv = 1.0 / (ROPE_THETA ** (jnp.arange(0, MLA_ROPE, 2, dtype=F32) / MLA_ROPE))
    ang = jnp.arange(seq, dtype=F32)[:, None] * inv[None, :]
    cos, sin = jnp.cos(ang), jnp.sin(ang)
    return jnp.concatenate([cos, cos, sin, sin], axis=1)


def _row(v):
    return v.reshape(1, -1).astype(F32)


def _pad_rows(w, top, total):
    return jnp.concatenate([jnp.zeros((top, w.shape[1]), w.dtype), w,
                            jnp.zeros((total - top - w.shape[0], w.shape[1]), w.dtype)], axis=0)


def _tiles(seq):
    return {
        "tm": min(512, seq),
        "tk": min(512, seq // 2),
        "tc": min(256, seq),
        "tr": min(128, seq),
        "ts": min(512, seq),
        "tf": 256,
    }


def kernel(x, ln1_g, w_in, mla_q_norm_g, mla_w_uq, mla_kv_norm_g, mla_w_ukv, mla_out_norm_g, gla_w_gk, gla_b_gk, gla_norm_g, rwkv_mu, rwkv_w0, rwkv_w2, rwkv_a0, rwkv_a2, rwkv_g2, rwkv_k_k, rwkv_k_a, rwkv_r_k, rwkv_ln_g, rwkv_ln_b, w_out, ln2_g, ffn_w_up, ffn_conv_w, ffn_conv_b, ffn_w_down, final_g):
    batch, seq, d_model = x.shape
    depth = w_in.shape[0]
    tl = _tiles(seq)
    masks = _mask_inputs()
    grp = masks["group64"]
    rope = _rope_table(seq)
    xt = x.reshape(batch * seq, d_model)
    for l in range(depth):
        w2p = _bf(_pad_rows(rwkv_w2[l], 0, LANES))
        a2p = _bf(_pad_rows(rwkv_a2[l], RWKV_DECAY_RANK, LANES))
        mla = (_row(mla_q_norm_g[l]), _row(mla_kv_norm_g[l]), _regroup_w_uq(mla_w_uq[l]),
               _regroup_w_ukv(mla_w_ukv[l]))
        rwkv = (_row(rwkv_mu[l]), _row(rwkv_w0[l]), w2p, _row(rwkv_a0[l]), a2p, _bf(rwkv_g2[l]),
                _row(rwkv_k_k[l]), _row(rwkv_k_a[l]), _row(rwkv_r_k[l]))
        q, k, v, pb, r, lw, k2, vv, kk, bvec, gate, bonus = _proj(
            xt, _row(ln1_g[l]), _regroup_w_in(w_in[l]), mla, rwkv, rope, grp, tm=tl["tm"], seq=seq)
        o_mla = _flash(q, k, v, batch=batch, seq=seq, tk=tl["tk"])
        wgk = _bf(_pad_rows(gla_w_gk[l], 0, LANES))
        o_gla = _gla(pb, wgk, _row(gla_b_gk[l]), masks, batch=batch, seq=seq, tc=tl["tc"])
        y_rwkv = _rwkv(r, lw, k2, vv, kk, bvec, masks, batch=batch, seq=seq, tc=tl["tr"])
        wo = _bf(w_out[l])
        mix_params = (_row(mla_out_norm_g[l]), _row(jnp.tile(gla_norm_g[l], GLA_HEADS)), _row(rwkv_ln_g[l]),
                      _row(rwkv_ln_b[l]), grp, wo[:MLA_WIDTH], wo[MLA_WIDTH:MLA_WIDTH + GLA_WIDTH],
                      wo[MLA_WIDTH + GLA_WIDTH:])
        xt = _mix_ffn(xt, (o_mla, o_gla, pb, y_rwkv, bonus, gate), mix_params, _row(ln2_g[l]), _bf(ffn_w_up[l]),
                      ffn_conv_w[l].astype(F32), _row(ffn_conv_b[l]), _bf(ffn_w_down[l]), _row(final_g),
                      batch=batch, seq=seq, ts=tl["ts"], tf=tl["tf"], final_norm=(l == depth - 1))
    return xt.reshape(batch, seq, d_model)
```

```python
import functools
import math

import numpy as np
import jax
import jax.numpy as jnp
from jax import lax
from jax.experimental import pallas as pl
from jax.experimental.pallas import tpu as pltpu

F32 = jnp.float32
BF16 = jnp.bfloat16

MLA_HEADS = 4
MLA_NOPE = 128
MLA_ROPE = 64
MLA_V = 128
MLA_QK = MLA_NOPE + MLA_ROPE
MLA_Q_LORA = 384
MLA_KV_LORA = 256
MLA_WIDTH = MLA_HEADS * MLA_V
ROPE_THETA = 10000.0
GLA_HEADS = 4
GLA_DK = 32
GLA_DV = 64
GLA_QK_WIDTH = GLA_HEADS * GLA_DK
GLA_WIDTH = GLA_HEADS * GLA_DV
GLA_GATE_RANK = 16
GLA_GATE_NORM = 16.0
RWKV_HEADS = 4
RWKV_N = 64
RWKV_WIDTH = RWKV_HEADS * RWKV_N
RWKV_DECAY_RANK = 64
RWKV_A_RANK = 64
RWKV_GATE_RANK = 128
RWKV_LN_EPS = 64e-5
MLA_COLS = MLA_Q_LORA + MLA_KV_LORA + MLA_ROPE
GLA_COLS = 2 * GLA_QK_WIDTH + GLA_WIDTH + GLA_GATE_RANK + GLA_WIDTH
RWKV_COLS = 3 * RWKV_WIDTH + RWKV_DECAY_RANK + RWKV_A_RANK + RWKV_GATE_RANK
NORM_EPS = 1e-6
CONV_WIDTH = 3

LANES = 128
SUBLANES = 8
VMEM_LIMIT = 56 * 1024 * 1024

PA_W = MLA_Q_LORA + MLA_KV_LORA + 2 * MLA_ROPE
PB_W = 2 * GLA_QK_WIDTH + 2 * GLA_WIDTH + LANES
PC_W = RWKV_COLS
QK_HEAD_W = 2 * LANES

LOG2E = math.log2(math.e)

CHUNK = 64


def _dot(a, b):
    return lax.dot_general(a, b, (((1,), (0,)), ((), ())), preferred_element_type=F32)


def _dot_nt(a, b):
    return lax.dot_general(a, b, (((1,), (1,)), ((), ())), preferred_element_type=F32)


def _dot_tn(a, b):
    return lax.dot_general(a, b, (((0,), (0,)), ((), ())), preferred_element_type=F32)


def _bf(x):
    return x.astype(BF16)


def _split_terms(x, n):
    terms = []
    rem = x
    for _ in range(n):
        t = rem.astype(BF16)
        terms.append(t)
        rem = rem - t.astype(F32)
    return terms


def _exact_left(m, x, n=3):
    out = None
    for t in _split_terms(x, n):
        y = _dot(m, t)
        out = y if out is None else out + y
    return out


def _exact_right(x, m, n=2):
    out = None
    for t in _split_terms(x, n):
        y = _dot(t, m)
        out = y if out is None else out + y
    return out


def _rms(x, g, eps=NORM_EPS):
    return x * lax.rsqrt(jnp.mean(x * x, axis=-1, keepdims=True) + eps) * g


def _sigmoid(x):
    return 1.0 / (1.0 + jnp.exp(-x))


def _softplus(x):
    return jnp.maximum(x, 0.0) + jnp.log(1.0 + jnp.exp(-jnp.abs(x)))


def _params(*sem):
    return pltpu.CompilerParams(dimension_semantics=sem, vmem_limit_bytes=VMEM_LIMIT)


def _proj_kernel(x_ref, g_ref, w_ref, gq_ref, gkv_ref, wq_ref, wkv_ref, rope_ref, mu_ref, w0_ref, w2_ref,
                 a0_ref, a2_ref, g2_ref, kk_ref, ka_ref, rk_ref, grp_ref,
                 q_out, k_out, v_out, pb_out, r_out, lw_out, kr_out, vr_out, kkr_out, b_out, gate_out, bonus_out,
                 carry_sc, *, blocks_per_seq, n_step):
    h = _bf(_rms(x_ref[...], g_ref[...]))

    def group(base, width):
        cols = [_dot(h, w_ref[:, base + c:base + min(c + n_step, width)]) for c in range(0, width, n_step)]
        return cols[0] if len(cols) == 1 else jnp.concatenate(cols, axis=1)

    _mla_prep_body(group(0, PA_W), gq_ref, gkv_ref, wq_ref, wkv_ref, rope_ref, q_out, k_out, v_out)
    pb_out[...] = _bf(group(PA_W, PB_W))
    pc = group(PA_W + PB_W, PC_W)
    first = (pl.program_id(0) % blocks_per_seq) == 0
    last_prev = jnp.where(first, 0.0, carry_sc[SUBLANES - 1:SUBLANES, :])
    carry_sc[...] = pc[pc.shape[0] - SUBLANES:, :]
    _rwkv_prep_body(pc, last_prev, mu_ref, w0_ref, w2_ref, a0_ref, a2_ref, g2_ref, kk_ref, ka_ref, rk_ref, grp_ref,
                    r_out, lw_out, kr_out, vr_out, kkr_out, b_out, gate_out, bonus_out)


def _proj(x, g, w, mla, rwkv, rope, grp, *, tm, seq):
    t, d = x.shape
    const = lambda i: (0, 0)
    resident = lambda a: pl.BlockSpec(a.shape, const, pipeline_mode=pl.Buffered(1))
    row = lambda width: pl.BlockSpec((tm, width), lambda i: (i, 0))
    sds = lambda width, dt: jax.ShapeDtypeStruct((t, width), dt)
    qk_w = MLA_HEADS * QK_HEAD_W
    n_seq_blocks = seq // tm
    return pl.pallas_call(
        functools.partial(_proj_kernel, blocks_per_seq=n_seq_blocks, n_step=512),
        grid=(t // tm,),
        in_specs=[row(d), pl.BlockSpec((1, d), const), resident(w)]
                 + [resident(a) for a in mla]
                 + [pl.BlockSpec((tm, LANES), lambda i: (i % n_seq_blocks, 0))]
                 + [resident(a) for a in rwkv] + [resident(grp)],
        out_specs=[row(qk_w), row(qk_w), row(2 * MLA_WIDTH), row(PB_W)] + [row(RWKV_WIDTH)] * 8,
        out_shape=[sds(qk_w, BF16), sds(qk_w, BF16), sds(2 * MLA_WIDTH, BF16), sds(PB_W, BF16),
                   sds(RWKV_WIDTH, BF16), sds(RWKV_WIDTH, F32)] + [sds(RWKV_WIDTH, BF16)] * 6,
        scratch_shapes=[pltpu.VMEM((SUBLANES, PC_W), F32)],
        compiler_params=_params("arbitrary"),
        name="proj",
    )(x, g, w, *mla, rope, *rwkv, grp)


def _mla_prep_body(pa, gq_ref, gkv_ref, wq_ref, wkv_ref, rope_ref, q_ref, k_ref, v_ref):
    rope = rope_ref[...]
    cq = _bf(_rms(pa[:, :MLA_Q_LORA], gq_ref[...]))
    q = _dot(cq, wq_ref[...]) * (MLA_QK ** -0.5 * LOG2E)
    for h in range(MLA_HEADS):
        b = h * QK_HEAD_W
        q_ref[:, b:b + LANES] = _bf(q[:, b:b + LANES])
        q_ref[:, b + LANES:b + QK_HEAD_W] = _bf(q[:, b + LANES:b + QK_HEAD_W] * rope)
    ckv = _bf(_rms(pa[:, MLA_Q_LORA:MLA_Q_LORA + MLA_KV_LORA], gkv_ref[...]))
    kv = _dot(ckv, wkv_ref[...])
    kr = pa[:, MLA_Q_LORA + MLA_KV_LORA:] * rope
    kr = _bf(kr + pltpu.roll(kr, MLA_ROPE, 1))
    for h in range(MLA_HEADS):
        b = h * QK_HEAD_W
        k_ref[:, b:b + LANES] = _bf(kv[:, h * MLA_NOPE:(h + 1) * MLA_NOPE])
        k_ref[:, b + LANES:b + QK_HEAD_W] = kr
    ones = jnp.ones((pa.shape[0], MLA_V), BF16)
    for h in range(MLA_HEADS):
        b = h * 2 * MLA_V
        v_ref[:, b:b + MLA_V] = _bf(kv[:, (MLA_HEADS + h) * MLA_NOPE:(MLA_HEADS + h + 1) * MLA_NOPE])
        v_ref[:, b + MLA_V:b + 2 * MLA_V] = ones


def _flash_kernel(q_ref, k_ref, v_ref, o_ref, m_sc, acc_sc, s_sc, *, tk):
    qi = pl.program_id(2)
    m_sc[...] = jnp.full(m_sc.shape, -jnp.inf, F32)
    acc_sc[...] = jnp.zeros(acc_sc.shape, F32)

    def kv_rows(j):
        return pl.ds(pl.multiple_of(j * tk, tk), tk)

    def scores(sub, j):
        return _dot_nt(q_ref[sub * tk:(sub + 1) * tk, :], k_ref[kv_rows(j), :])

    def accumulate(sub, j, s, masked):
        rows = slice(sub * tk, (sub + 1) * tk)
        if masked:
            row = lax.broadcasted_iota(jnp.int32, s.shape, 0)
            col = lax.broadcasted_iota(jnp.int32, s.shape, 1)
            s = jnp.where(row >= col, s, -jnp.inf)
        m_prev = m_sc[rows, :]
        m_next = jnp.maximum(m_prev, jnp.max(s, axis=1, keepdims=True))
        alpha = jnp.exp2(m_prev - m_next)
        p = jnp.exp2(s - jnp.concatenate([m_next] * (tk // LANES), axis=1))
        acc_sc[rows, :] = (acc_sc[rows, :] * jnp.concatenate([alpha, alpha], axis=1)
                           + _dot(_bf(p), v_ref[kv_rows(j), :]))
        m_sc[rows, :] = m_next

    for sub in range(2):
        s_sc[0, sub] = scores(sub, 0)

    def body(jj, carry):
        j = 2 * jj
        for slot in range(2):
            for sub in range(2):
                s = s_sc[slot, sub]
                s_sc[1 - slot, sub] = scores(sub, j + slot + 1)
                accumulate(sub, j + slot, s, False)
        return carry

    lax.fori_loop(0, qi, body, 0)
    j = 2 * qi
    s_last = scores(1, j + 1)
    accumulate(0, j, s_sc[0, 0], True)
    accumulate(1, j, s_sc[0, 1], False)
    accumulate(1, j + 1, s_last, True)
    acc = acc_sc[...]
    o_ref[...] = _bf(acc[:, :MLA_V] / acc[:, MLA_V:])


def _flash(q, k, v1, *, batch, seq, tk):
    tq = 2 * tk
    nq = seq // tq
    return pl.pallas_call(
        functools.partial(_flash_kernel, tk=tk),
        grid=(batch, MLA_HEADS, nq),
        in_specs=[pl.BlockSpec((tq, QK_HEAD_W), lambda b, h, i: (b * nq + i, h)),
                  pl.BlockSpec((seq, QK_HEAD_W), lambda b, h, i: (b, h)),
                  pl.BlockSpec((seq, 2 * MLA_V), lambda b, h, i: (b, h))],
        out_specs=pl.BlockSpec((tq, MLA_V), lambda b, h, i: (b * nq + i, h)),
        out_shape=jax.ShapeDtypeStruct((batch * seq, MLA_WIDTH), BF16),
        scratch_shapes=[pltpu.VMEM((tq, LANES), F32),
                        pltpu.VMEM((tq, 2 * MLA_V), F32),
                        pltpu.VMEM((2, 2, tk, tk), F32)],
        compiler_params=_params("parallel", "parallel", "arbitrary"),
        name="flash",
    )(q, k, v1)


def _np_masks():
    c = CHUNK
    r4 = np.arange(4 * c)
    tri = (np.arange(c)[:, None] >= np.arange(c)[None, :])
    m = {
        "tri": tri,
        "head64": (r4[:, None] // c) == (np.arange(RWKV_WIDTH)[None, :] // RWKV_N),
        "head32": (r4[:, None] // c) == (np.arange(GLA_QK_WIDTH)[None, :] // GLA_DK),
        "strict": np.arange(c)[:, None] > (r4[None, :] % c),
        "causal": np.arange(c)[:, None] >= (r4[None, :] % c),
    }
    return m


def _mask_inputs():
    m = _np_masks()
    return {
        "tri": jnp.asarray(m["tri"], BF16),
        "head64": jnp.asarray(m["head64"], F32),
        "head32": jnp.asarray(m["head32"], F32),
        "strict": jnp.asarray(m["strict"], F32),
        "causal": jnp.asarray(m["causal"], F32),
        "group64": jnp.asarray(m["head64"], BF16),
    }


def _stack(x, head_mask):
    return jnp.concatenate([x] * 4, axis=0) * head_mask


def _gla_kernel(qk_ref, v_ref, glo_ref, wgk_ref, bgk_ref, tri_ref, h32_ref, h64_ref, causal_ref,
                o_ref, st_sc, *, n_chunks):
    @pl.when(pl.program_id(0) == 0)
    def _():
        st_sc[...] = jnp.zeros(st_sc.shape, F32)

    c = CHUNK
    tri = tri_ref[...]
    h32 = h32_ref[...]
    h64 = h64_ref[...]
    causal = causal_ref[...] != 0
    nb = qk_ref.shape[0]
    items = [(b, slice(j * c, (j + 1) * c)) for j in range(n_chunks) for b in range(nb)]
    pre = [_dot(glo_ref[b], wgk_ref[...]) + bgk_ref[...] for b in range(nb)]
    gk = [-_softplus(-pre[b][rows, :]) / GLA_GATE_NORM for b, rows in items]
    q = [qk_ref[b, rows, :GLA_QK_WIDTH].astype(F32) for b, rows in items]
    k = [qk_ref[b, rows, GLA_QK_WIDTH:].astype(F32) for b, rows in items]
    v = [v_ref[b, rows, :] for b, rows in items]
    bc = _each(lambda x: _exact_left(tri, x), gk)
    b_last = _each(lambda x: x[c - 1:c, :], bc)
    q_t = _each(lambda q_, b_: q_ * (GLA_DK ** -0.5) * jnp.exp(b_), q, bc)
    k_t = _each(lambda k_, b_: k_ * jnp.exp(-b_), k, bc)
    k_end = _each(lambda k_, bl, b_: k_ * jnp.exp(bl - b_), k, b_last, bc)
    a = _each(lambda q_, k_: jnp.where(causal, _dot_nt(_bf(q_), _bf(_stack(k_, h32))), 0.0), q_t, k_t)
    o_intra = _each(lambda a_, v_: _dot(_bf(a_), _bf(_stack(v_, h64))), a, v)
    upd = _each(lambda v_, ke: _dot_tn(_bf(v_), _bf(ke)) * h32, v, k_end)
    decay = _each(jnp.exp, b_last)
    st = [st_sc[b] for b in range(nb)]
    for i, (b, rows) in enumerate(items):
        o_ref[b, rows, :] = o_intra[i] + _dot_nt(_bf(q_t[i]), _bf(st[b]))
        st[b] = st[b] * decay[i] + upd[i]
    for b in range(nb):
        st_sc[b] = st[b]


def _gla(pb, wgk, bgk, masks, *, batch, seq, tc):
    t = pb.shape[0]
    const = lambda s: (0, 0)
    pb3 = pb.reshape(batch, seq, pb.shape[1])
    o = pl.pallas_call(
        functools.partial(_gla_kernel, n_chunks=tc // CHUNK),
        grid=(seq // tc,),
        in_specs=[pl.BlockSpec((batch, tc, 2 * GLA_QK_WIDTH), lambda s: (0, s, 0)),
                  pl.BlockSpec((batch, tc, GLA_WIDTH), lambda s: (0, s, 1)),
                  pl.BlockSpec((batch, tc, LANES), lambda s: (0, s, 6)),
                  pl.BlockSpec(wgk.shape, const),
                  pl.BlockSpec(bgk.shape, const),
                  pl.BlockSpec(masks["tri"].shape, const),
                  pl.BlockSpec(masks["head32"].shape, const),
                  pl.BlockSpec(masks["head64"].shape, const),
                  pl.BlockSpec(masks["causal"].shape, const)],
        out_specs=pl.BlockSpec((batch, tc, GLA_WIDTH), lambda s: (0, s, 0)),
        out_shape=jax.ShapeDtypeStruct((batch, seq, GLA_WIDTH), F32),
        scratch_shapes=[pltpu.VMEM((batch, GLA_WIDTH, GLA_QK_WIDTH), F32)],
        compiler_params=_params("arbitrary"),
        name="gla",
    )(pb3, pb3, pb3, wgk, bgk, masks["tri"], masks["head32"], masks["head64"], masks["causal"])
    return o.reshape(t, GLA_WIDTH)


def _rwkv_prep_body(xc, last_prev, mu_ref, w0_ref, w2_ref, a0_ref, a2_ref, g2_ref, kk_ref, ka_ref,
                    rk_ref, grp_ref, r_out, lw_out, k_out, v_out, kk_out, b_out, g_out, bonus_out):
    row = lax.broadcasted_iota(jnp.int32, xc.shape, 0)
    prev = jnp.where(row == 0, last_prev, pltpu.roll(xc, 1, 0))
    xm = xc + (prev - xc) * mu_ref[...]
    w3 = RWKV_WIDTH
    r = xm[:, 0:w3]
    k = xm[:, w3:2 * w3]
    v = xm[:, 2 * w3:3 * w3]
    wa = xm[:, 3 * w3:3 * w3 + LANES]
    g_lo = xm[:, 3 * w3 + LANES:]
    w = -_softplus(-(w0_ref[...] + _dot(_bf(jnp.tanh(wa)), w2_ref[...]))) - 0.5
    a = _sigmoid(a0_ref[...] + _dot(_bf(wa), a2_ref[...]))
    g = _dot(_bf(_sigmoid(g_lo)), g2_ref[...])
    grp = grp_ref[...]
    kk = k * kk_ref[...]
    norm = jnp.sqrt(_exact_right(kk * kk, grp))
    kk = kk / jnp.maximum(norm, 1e-12)
    k2 = k * (1.0 + (a - 1.0) * ka_ref[...])
    r_out[...] = _bf(r)
    lw_out[...] = -jnp.exp(w)
    k_out[...] = _bf(k2)
    v_out[...] = _bf(v)
    kk_out[...] = _bf(kk)
    b_out[...] = _bf(kk * a)
    g_out[...] = _bf(g)
    bonus_out[...] = _bf(_exact_right(r * k2 * rk_ref[...], grp) * v)


def _each(fn, *lists):
    return [fn(*xs) for xs in zip(*lists)]


def _rwkv_pre(r, lw, k2, v, kk, bvec, tri, h64, strict, causal):
    c = CHUNK
    g = _each(lambda x: _exact_left(tri, x), lw)
    g_last = _each(lambda x: x[c - 1:c, :], g)
    e_neg = _each(lambda x: jnp.exp(-x), g)
    a_t = _each(lambda kk_, g_, lw_: -kk_ * jnp.exp(g_ - lw_), kk, g, lw)
    r_t = _each(lambda r_, g_: r_ * jnp.exp(g_), r, g)
    b_t = _each(jnp.multiply, bvec, e_neg)
    k_t = _each(jnp.multiply, k2, e_neg)
    e_end = _each(lambda gl, g_: jnp.exp(gl - g_), g_last, g)
    b_end = _each(jnp.multiply, bvec, e_end)
    k_end = _each(jnp.multiply, k2, e_end)

    h64b = _bf(h64)
    bd = lambda x: _stack(_bf(x), h64b)
    a_s = _each(bd, a_t)
    bk_s = _each(lambda b_, k_: jnp.concatenate([bd(b_), bd(k_)], axis=0), b_t, k_t)
    v_s = _each(bd, v)
    ar = _each(lambda a_, r_: _bf(jnp.concatenate([a_, r_], axis=0)), a_t, r_t)
    prod = _each(_dot_nt, ar, bk_s)
    l1 = _each(lambda x: jnp.where(strict, x[:c, :4 * c], 0.0), prod)
    a_ak = _each(lambda x: jnp.where(strict, x[:c, 4 * c:], 0.0), prod)
    a_rb = _each(lambda x: jnp.where(causal, x[c:, :4 * c], 0.0), prod)
    a_rk = _each(lambda x: jnp.where(causal, x[c:, 4 * c:], 0.0), prod)

    shape = l1[0].shape
    eye = (lax.broadcasted_iota(jnp.int32, shape, 0)
           == lax.broadcasted_iota(jnp.int32, shape, 1) % c).astype(F32)
    t_inv = _each(lambda x: eye + x, l1)
    lp = l1
    lp_bd = _each(bd, lp)
    for _ in range(5):
        lp = _each(lambda x, xb: _dot(_bf(x), xb), lp, lp_bd)
        lp_bd = _each(bd, lp)
        t_inv = _each(lambda t, pb: t + _dot(_bf(t), pb), t_inv, lp_bd)
    t_b = _each(_bf, t_inv)
    av_yv = _each(lambda a, ark, vs: _dot(_bf(jnp.concatenate([a, ark], axis=0)), vs), a_ak, a_rk, v_s)
    y_v = _each(lambda x: x[c:], av_yv)
    w_a = _each(_dot, t_b, a_s)
    u_v = _each(lambda t, x: _dot(t, bd(x[:c])), t_b, av_yv)

    bk_end = _each(lambda b_, k_: _bf(jnp.concatenate([b_, k_], axis=0)), b_end, k_end)
    decay = _each(jnp.exp, g_last)
    wr = _each(lambda w, r_: _bf(jnp.concatenate([w, r_], axis=0)), w_a, r_t)
    return {"wr": wr, "u_v": u_v, "a_rb": _each(_bf, a_rb), "y_v": y_v, "v": v, "bk_end": bk_end, "decay": decay}


def _rwkv_post(p, items, s, h64):
    c = CHUNK
    pick = lambda name: [p[name][i] for i in items]
    ur = _each(lambda wr, s_: _dot_nt(wr, _bf(s_)), pick("wr"), s)
    u = _each(lambda x, uv_: x[:c] + uv_, ur, pick("u_v"))
    h64b = _bf(h64)
    y = _each(lambda x, arb, u_, yv: x[c:] + _dot(arb, _stack(_bf(u_), h64b)) + yv,
              ur, pick("a_rb"), u, pick("y_v"))
    uv = _each(lambda u_, v_: _bf(jnp.concatenate([u_, v_], axis=0)), u, pick("v"))
    s_new = _each(lambda s_, d, x, z: s_ * d + _dot_tn(x, z) * h64, s, pick("decay"), uv, pick("bk_end"))
    return y, s_new


def _rwkv_kernel(r_ref, lw_ref, k_ref, v_ref, kk_ref, b_ref, tri_ref, h64_ref, strict_ref, causal_ref,
                 y_ref, s_sc, *, n_chunks):
    @pl.when(pl.program_id(0) == 0)
    def _():
        s_sc[...] = jnp.zeros(s_sc.shape, F32)

    tri = tri_ref[...]
    h64 = h64_ref[...]
    strict = strict_ref[...] != 0
    causal = causal_ref[...] != 0
    nb = r_ref.shape[0]
    c = CHUNK
    where = [(b, slice(j * c, (j + 1) * c)) for j in range(n_chunks) for b in range(nb)]
    load = lambda ref: [ref[b, rows, :].astype(F32) for b, rows in where]
    p = _rwkv_pre(load(r_ref), load(lw_ref), load(k_ref), load(v_ref), load(kk_ref), load(b_ref),
                  tri, h64, strict, causal)
    s = [s_sc[b] for b in range(nb)]
    for j in range(n_chunks):
        items = list(range(j * nb, (j + 1) * nb))
        y, s = _rwkv_post(p, items, s, h64)
        for b in range(nb):
            y_ref[b, j * c:(j + 1) * c, :] = y[b]
    for b in range(nb):
        s_sc[b] = s[b]


def _rwkv(r, lw, k2, v, kk, bvec, masks, *, batch, seq, tc):
    t = r.shape[0]
    const = lambda s: (0, 0)
    spec = pl.BlockSpec((batch, tc, RWKV_WIDTH), lambda s: (0, s, 0))
    seqs = [a.reshape(batch, seq, RWKV_WIDTH) for a in (r, lw, k2, v, kk, bvec)]
    y = pl.pallas_call(
        functools.partial(_rwkv_kernel, n_chunks=tc // CHUNK),
        grid=(seq // tc,),
        in_specs=[spec] * 6 + [pl.BlockSpec(masks["tri"].shape, const),
                               pl.BlockSpec(masks["head64"].shape, const),
                               pl.BlockSpec(masks["strict"].shape, const),
                               pl.BlockSpec(masks["causal"].shape, const)],
        out_specs=spec,
        out_shape=jax.ShapeDtypeStruct((batch, seq, RWKV_WIDTH), F32),
        scratch_shapes=[pltpu.VMEM((batch, RWKV_WIDTH, RWKV_WIDTH), F32)],
        compiler_params=_params("arbitrary"),
        name="rwkv",
    )(*seqs, masks["tri"], masks["head64"], masks["strict"], masks["causal"])
    return y.reshape(t, RWKV_WIDTH)


def _mix_out_body(x, oa_ref, ob_ref, gout_ref, yc_ref, bonus_ref, gate_ref, na_ref, nb_ref,
                  lng_ref, lnb_ref, grp_ref, wa_ref, wb_ref, wc_ref):
    grp = grp_ref[...]
    ya = _rms(oa_ref[...].astype(F32), na_ref[...])
    ob = ob_ref[...]
    ms = _exact_right(ob * ob, grp) * (1.0 / GLA_DV)
    gout = gout_ref[...].astype(F32)
    yb = ob * lax.rsqrt(ms + NORM_EPS) * nb_ref[...] * (gout * _sigmoid(gout))
    y = yc_ref[...]
    mean = _exact_right(y, grp) * (1.0 / RWKV_N)
    d = y - mean
    var = _exact_right(d * d, grp) * (1.0 / RWKV_N)
    yc = (d * lax.rsqrt(var + RWKV_LN_EPS) * lng_ref[...] + lnb_ref[...] + bonus_ref[...].astype(F32)) * gate_ref[...].astype(F32)
    return x + _dot(_bf(ya), wa_ref[...]) + _dot(_bf(yb), wb_ref[...]) + _dot(_bf(yc), wc_ref[...])


def _shift_rows(y, carry):
    r = pltpu.roll(y, 1, 0)
    sub = lax.broadcasted_iota(jnp.int32, carry.shape, 0)
    head = jnp.where(sub == 0, carry[SUBLANES - 1:SUBLANES, :], r[:SUBLANES])
    return jnp.concatenate([head, r[SUBLANES:]], axis=0)


def _causal_conv3(u, prev, cw, cb):
    c0, c1, c2 = cw[0:1, :], cw[1:2, :], cw[2:3, :]
    z_prev = c0 * prev
    w_prev = c1 * prev + pltpu.roll(z_prev, 1, 0)
    w = c1 * u + _shift_rows(c0 * u, z_prev)
    return c2 * u + cb + _shift_rows(w, w_prev)


def _mix_ffn_kernel(x_ref, oa_ref, ob_ref, gout_ref, yc_ref, bonus_ref, gate_ref, na_ref, nb_ref, lng_ref, lnb_ref,
                    grp_ref, wa_ref, wb_ref, wc_ref, g_ref, wu_ref, cw_ref, cb_ref, wd_ref, fg_ref,
                    o_ref, cg_sc, cv_sc, *, d_ff, tf, final_norm):
    x = _mix_out_body(x_ref[...], oa_ref, ob_ref, gout_ref, yc_ref, bonus_ref, gate_ref, na_ref, nb_ref,
                      lng_ref, lnb_ref, grp_ref, wa_ref, wb_ref, wc_ref)
    h = _bf(_rms(x, g_ref[...]))
    ts = h.shape[0]
    first = pl.program_id(1) == 0
    n_f = d_ff // tf

    def up(c):
        return (_dot(h, wu_ref[:, c * tf:(c + 1) * tf]),
                _dot(h, wu_ref[:, d_ff + c * tf:d_ff + (c + 1) * tf]))

    acc = None
    nxt = up(0)
    for c in range(n_f):
        ug, uv = nxt
        if c + 1 < n_f:
            nxt = up(c + 1)
        gcol = slice(c * tf, (c + 1) * tf)
        vcol = slice(d_ff + c * tf, d_ff + (c + 1) * tf)
        gate = _causal_conv3(ug, jnp.where(first, 0.0, cg_sc[c]), cw_ref[:, gcol], cb_ref[:, gcol])
        val = _causal_conv3(uv, jnp.where(first, 0.0, cv_sc[c]), cw_ref[:, vcol], cb_ref[:, vcol])
        cg_sc[c] = ug[ts - SUBLANES:, :]
        cv_sc[c] = uv[ts - SUBLANES:, :]
        part = _dot(_bf(gate * _sigmoid(gate) * val), wd_ref[gcol, :])
        acc = part if acc is None else acc + part
    y = x + acc
    if final_norm:
        y = _rms(y, fg_ref[...])
    o_ref[...] = y


def _mix_ffn(x, mixers, mix_params, g, w_up, conv_w, conv_b, w_down, final_g, *, batch, seq, ts, tf, final_norm):
    t, d = x.shape
    d_ff = w_down.shape[0]
    nsb = seq // ts
    const = lambda b, s: (0, 0)
    rows = lambda width, col=0: pl.BlockSpec((ts, width), lambda b, s: (b * nsb + s, col))
    resident = lambda a: pl.BlockSpec(a.shape, const, pipeline_mode=pl.Buffered(1))
    return pl.pallas_call(
        functools.partial(_mix_ffn_kernel, d_ff=d_ff, tf=tf, final_norm=final_norm),
        grid=(batch, nsb),
        in_specs=[rows(d), rows(MLA_WIDTH), rows(GLA_WIDTH),
                  rows(GLA_WIDTH, 2),
                  rows(RWKV_WIDTH), rows(RWKV_WIDTH), rows(RWKV_WIDTH)]
                 + [resident(a) for a in mix_params]
                 + [pl.BlockSpec((1, d), const),
                    resident(w_up), resident(conv_w), resident(conv_b), resident(w_down),
                    pl.BlockSpec((1, d), const)],
        out_specs=rows(d),
        out_shape=jax.ShapeDtypeStruct((t, d), F32),
        scratch_shapes=[pltpu.VMEM((d_ff // tf, SUBLANES, tf), F32),
                        pltpu.VMEM((d_ff // tf, SUBLANES, tf), F32)],
        compiler_params=_params("parallel", "arbitrary"),
        name="mix_ffn",
    )(x, *mixers, *mix_params, g, w_up, conv_w, conv_b, w_down, final_g)


def _rot_cols(w):
    half = w.shape[-1] // 2
    return jnp.concatenate([-w[..., half:], w[..., :half]], axis=-1)


def _regroup_w_in(w_in):
    d = w_in.shape[0]
    a0 = 0
    b0 = MLA_COLS
    c0 = MLA_COLS + GLA_COLS
    k_pe = w_in[:, MLA_Q_LORA + MLA_KV_LORA:MLA_COLS]
    gq = w_in[:, b0:b0 + GLA_QK_WIDTH]
    gk = w_in[:, b0 + GLA_QK_WIDTH:b0 + 2 * GLA_QK_WIDTH]
    gv = w_in[:, b0 + 2 * GLA_QK_WIDTH:b0 + 2 * GLA_QK_WIDTH + GLA_WIDTH]
    glo = w_in[:, b0 + 2 * GLA_QK_WIDTH + GLA_WIDTH:b0 + 2 * GLA_QK_WIDTH + GLA_WIDTH + GLA_GATE_RANK]
    gout = w_in[:, b0 + 2 * GLA_QK_WIDTH + GLA_WIDTH + GLA_GATE_RANK:c0]
    pad = jnp.zeros((d, LANES - GLA_GATE_RANK), w_in.dtype)
    cols = [w_in[:, a0:MLA_COLS], _rot_cols(k_pe), gq, gk, gv, gout, glo, pad, w_in[:, c0:]]
    return _bf(jnp.concatenate(cols, axis=1))


def _regroup_w_uq(w_uq):
    r = w_uq.shape[0]
    w = w_uq.reshape(r, MLA_HEADS, MLA_QK)
    nope = w[:, :, :MLA_NOPE]
    pe = w[:, :, MLA_NOPE:]
    return _bf(jnp.concatenate([nope, pe, _rot_cols(pe)], axis=-1).reshape(r, MLA_HEADS * QK_HEAD_W))


def _regroup_w_ukv(w_ukv):
    r = w_ukv.shape[0]
    w = w_ukv.reshape(r, MLA_HEADS, MLA_NOPE + MLA_V)
    k_nope = w[:, :, :MLA_NOPE].reshape(r, MLA_HEADS * MLA_NOPE)
    v = w[:, :, MLA_NOPE:].reshape(r, MLA_HEADS * MLA_V)
    return _bf(jnp.concatenate([k_nope, v], axis=1))


def _rope_table(seq):
    inv = 1.0 / (ROPE_THETA ** (jnp.arange(0, MLA_ROPE, 2, dtype=F32) / MLA_ROPE))
    ang = jnp.arange(seq, dtype=F32)[:, None] * inv[None, :]
    cos, sin = jnp.cos(ang), jnp.sin(ang)
    return jnp.concatenate([cos, cos, sin, sin], axis=1)


def _row(v):
    return v.reshape(1, -1).astype(F32)


def _pad_rows(w, top, total):
    return jnp.concatenate([jnp.zeros((top, w.shape[1]), w.dtype), w,
                            jnp.zeros((total - top - w.shape[0], w.shape[1]), w.dtype)], axis=0)


def _tiles(seq):
    return {
        "tm": min(512, seq),
        "tk": min(512, seq // 2),
        "tc": min(256, seq),
        "tr": min(256, seq),
        "ts": min(512, seq),
        "tf": 256,
    }


def kernel(x, ln1_g, w_in, mla_q_norm_g, mla_w_uq, mla_kv_norm_g, mla_w_ukv, mla_out_norm_g, gla_w_gk, gla_b_gk, gla_norm_g, rwkv_mu, rwkv_w0, rwkv_w2, rwkv_a0, rwkv_a2, rwkv_g2, rwkv_k_k, rwkv_k_a, rwkv_r_k, rwkv_ln_g, rwkv_ln_b, w_out, ln2_g, ffn_w_up, ffn_conv_w, ffn_conv_b, ffn_w_down, final_g):
    batch, seq, d_model = x.shape
    depth = w_in.shape[0]
    tl = _tiles(seq)
    masks = _mask_inputs()
    grp = masks["group64"]
    rope = _rope_table(seq)
    xt = x.reshape(batch * seq, d_model)
    for l in range(depth):
        w2p = _bf(_pad_rows(rwkv_w2[l], 0, LANES))
        a2p = _bf(_pad_rows(rwkv_a2[l], RWKV_DECAY_RANK, LANES))
        mla = (_row(mla_q_norm_g[l]), _row(mla_kv_norm_g[l]), _regroup_w_uq(mla_w_uq[l]),
               _regroup_w_ukv(mla_w_ukv[l]))
        rwkv = (_row(rwkv_mu[l]), _row(rwkv_w0[l]), w2p, _row(rwkv_a0[l]), a2p, _bf(rwkv_g2[l]),
                _row(rwkv_k_k[l]), _row(rwkv_k_a[l]), _row(rwkv_r_k[l]))
        q, k, v, pb, r, lw, k2, vv, kk, bvec, gate, bonus = _proj(
            xt, _row(ln1_g[l]), _regroup_w_in(w_in[l]), mla, rwkv, rope, grp, tm=tl["tm"], seq=seq)
        o_mla = _flash(q, k, v, batch=batch, seq=seq, tk=tl["tk"])
        wgk = _bf(_pad_rows(gla_w_gk[l], 0, LANES))
        o_gla = _gla(pb, wgk, _row(gla_b_gk[l]), masks, batch=batch, seq=seq, tc=tl["tc"])
        y_rwkv = _rwkv(r, lw, k2, vv, kk, bvec, masks, batch=batch, seq=seq, tc=tl["tr"])
        wo = _bf(w_out[l])
        mix_params = (_row(mla_out_norm_g[l]), _row(jnp.tile(gla_norm_g[l], GLA_HEADS)), _row(rwkv_ln_g[l]),
                      _row(rwkv_ln_b[l]), grp, wo[:MLA_WIDTH], wo[MLA_WIDTH:MLA_WIDTH + GLA_WIDTH],
                      wo[MLA_WIDTH + GLA_WIDTH:])
        xt = _mix_ffn(xt, (o_mla, o_gla, pb, y_rwkv, bonus, gate), mix_params, _row(ln2_g[l]), _bf(ffn_w_up[l]),
                      ffn_conv_w[l].astype(F32), _row(ffn_conv_b[l]), _bf(ffn_w_down[l]), _row(final_g),
                      batch=batch, seq=seq, ts=tl["ts"], tf=tl["tf"], final_norm=(l == depth - 1))
    return xt.reshape(batch, seq, d_model)
```

```python
import functools
import math

import numpy as np
import jax
import jax.numpy as jnp
from jax import lax
from jax.experimental import pallas as pl
from jax.experimental.pallas import tpu as pltpu

F32 = jnp.float32
BF16 = jnp.bfloat16

MLA_HEADS = 4
MLA_NOPE = 128
MLA_ROPE = 64
MLA_V = 128
MLA_QK = MLA_NOPE + MLA_ROPE
MLA_Q_LORA = 384
MLA_KV_LORA = 256
MLA_WIDTH = MLA_HEADS * MLA_V
ROPE_THETA = 10000.0
GLA_HEADS = 4
GLA_DK = 32
GLA_DV = 64
GLA_QK_WIDTH = GLA_HEADS * GLA_DK
GLA_WIDTH = GLA_HEADS * GLA_DV
GLA_GATE_RANK = 16
GLA_GATE_NORM = 16.0
RWKV_HEADS = 4
RWKV_N = 64
RWKV_WIDTH = RWKV_HEADS * RWKV_N
RWKV_DECAY_RANK = 64
RWKV_A_RANK = 64
RWKV_GATE_RANK = 128
RWKV_LN_EPS = 64e-5
MLA_COLS = MLA_Q_LORA + MLA_KV_LORA + MLA_ROPE
GLA_COLS = 2 * GLA_QK_WIDTH + GLA_WIDTH + GLA_GATE_RANK + GLA_WIDTH
RWKV_COLS = 3 * RWKV_WIDTH + RWKV_DECAY_RANK + RWKV_A_RANK + RWKV_GATE_RANK
NORM_EPS = 1e-6
CONV_WIDTH = 3

LANES = 128
SUBLANES = 8
VMEM_LIMIT = 56 * 1024 * 1024

PA_W = MLA_Q_LORA + MLA_KV_LORA + 2 * MLA_ROPE
PB_W = 2 * GLA_QK_WIDTH + 2 * GLA_WIDTH + LANES
PC_W = RWKV_COLS
QK_HEAD_W = 2 * LANES

LOG2E = math.log2(math.e)

CHUNK = 64


def _dot(a, b):
    return lax.dot_general(a, b, (((1,), (0,)), ((), ())), preferred_element_type=F32)


def _dot_nt(a, b):
    return lax.dot_general(a, b, (((1,), (1,)), ((), ())), preferred_element_type=F32)


def _dot_tn(a, b):
    return lax.dot_general(a, b, (((0,), (0,)), ((), ())), preferred_element_type=F32)


def _bf(x):
    return x.astype(BF16)


def _split_terms(x, n):
    terms = []
    rem = x
    for _ in range(n):
        t = rem.astype(BF16)
        terms.append(t)
        rem = rem - t.astype(F32)
    return terms


def _exact_left(m, x, n=3):
    out = None
    for t in _split_terms(x, n):
        y = _dot(m, t)
        out = y if out is None else out + y
    return out


def _exact_right(x, m, n=2):
    out = None
    for t in _split_terms(x, n):
        y = _dot(t, m)
        out = y if out is None else out + y
    return out


def _rms(x, g, eps=NORM_EPS):
    return x * lax.rsqrt(jnp.mean(x * x, axis=-1, keepdims=True) + eps) * g


def _sigmoid(x):
    return 1.0 / (1.0 + jnp.exp(-x))


def _softplus(x):
    return jnp.maximum(x, 0.0) + jnp.log(1.0 + jnp.exp(-jnp.abs(x)))


def _params(*sem):
    return pltpu.CompilerParams(dimension_semantics=sem, vmem_limit_bytes=VMEM_LIMIT)


def _proj_kernel(x_ref, g_ref, w_ref, gq_ref, gkv_ref, wq_ref, wkv_ref, rope_ref, mu_ref, w0_ref, w2_ref,
                 a0_ref, a2_ref, g2_ref, kk_ref, ka_ref, rk_ref, grp_ref,
                 q_out, k_out, v_out, pb_out, r_out, lw_out, kr_out, vr_out, kkr_out, b_out, gate_out, bonus_out,
                 carry_sc, *, blocks_per_seq, n_step):
    h = _bf(_rms(x_ref[...], g_ref[...]))

    def group(base, width):
        cols = [_dot(h, w_ref[:, base + c:base + min(c + n_step, width)]) for c in range(0, width, n_step)]
        return cols[0] if len(cols) == 1 else jnp.concatenate(cols, axis=1)

    _mla_prep_body(group(0, PA_W), gq_ref, gkv_ref, wq_ref, wkv_ref, rope_ref, q_out, k_out, v_out)
    pb_out[...] = _bf(group(PA_W, PB_W))
    pc = group(PA_W + PB_W, PC_W)
    first = (pl.program_id(0) % blocks_per_seq) == 0
    last_prev = jnp.where(first, 0.0, carry_sc[SUBLANES - 1:SUBLANES, :])
    carry_sc[...] = pc[pc.shape[0] - SUBLANES:, :]
    _rwkv_prep_body(pc, last_prev, mu_ref, w0_ref, w2_ref, a0_ref, a2_ref, g2_ref, kk_ref, ka_ref, rk_ref, grp_ref,
                    r_out, lw_out, kr_out, vr_out, kkr_out, b_out, gate_out, bonus_out)


def _proj(x, g, w, mla, rwkv, rope, grp, *, tm, seq):
    t, d = x.shape
    const = lambda i: (0, 0)
    resident = lambda a: pl.BlockSpec(a.shape, const, pipeline_mode=pl.Buffered(1))
    row = lambda width: pl.BlockSpec((tm, width), lambda i: (i, 0))
    sds = lambda width, dt: jax.ShapeDtypeStruct((t, width), dt)
    qk_w = MLA_HEADS * QK_HEAD_W
    n_seq_blocks = seq // tm
    return pl.pallas_call(
        functools.partial(_proj_kernel, blocks_per_seq=n_seq_blocks, n_step=512),
        grid=(t // tm,),
        in_specs=[row(d), pl.BlockSpec((1, d), const), resident(w)]
                 + [resident(a) for a in mla]
                 + [pl.BlockSpec((tm, LANES), lambda i: (i % n_seq_blocks, 0))]
                 + [resident(a) for a in rwkv] + [resident(grp)],
        out_specs=[row(qk_w), row(qk_w), row(2 * MLA_WIDTH), row(PB_W)] + [row(RWKV_WIDTH)] * 8,
        out_shape=[sds(qk_w, BF16), sds(qk_w, BF16), sds(2 * MLA_WIDTH, BF16), sds(PB_W, BF16),
                   sds(RWKV_WIDTH, BF16), sds(RWKV_WIDTH, F32)] + [sds(RWKV_WIDTH, BF16)] * 6,
        scratch_shapes=[pltpu.VMEM((SUBLANES, PC_W), F32)],
        compiler_params=_params("arbitrary"),
        name="proj",
    )(x, g, w, *mla, rope, *rwkv, grp)


def _mla_prep_body(pa, gq_ref, gkv_ref, wq_ref, wkv_ref, rope_ref, q_ref, k_ref, v_ref):
    rope = rope_ref[...]
    cq = _bf(_rms(pa[:, :MLA_Q_LORA], gq_ref[...]))
    q = _dot(cq, wq_ref[...]) * (MLA_QK ** -0.5 * LOG2E)
    for h in range(MLA_HEADS):
        b = h * QK_HEAD_W
        q_ref[:, b:b + LANES] = _bf(q[:, b:b + LANES])
        q_ref[:, b + LANES:b + QK_HEAD_W] = _bf(q[:, b + LANES:b + QK_HEAD_W] * rope)
    ckv = _bf(_rms(pa[:, MLA_Q_LORA:MLA_Q_LORA + MLA_KV_LORA], gkv_ref[...]))
    kv = _dot(ckv, wkv_ref[...])
    kr = pa[:, MLA_Q_LORA + MLA_KV_LORA:] * rope
    kr = _bf(kr + pltpu.roll(kr, MLA_ROPE, 1))
    for h in range(MLA_HEADS):
        b = h * QK_HEAD_W
        k_ref[:, b:b + LANES] = _bf(kv[:, h * MLA_NOPE:(h + 1) * MLA_NOPE])
        k_ref[:, b + LANES:b + QK_HEAD_W] = kr
    ones = jnp.ones((pa.shape[0], MLA_V), BF16)
    for h in range(MLA_HEADS):
        b = h * 2 * MLA_V
        v_ref[:, b:b + MLA_V] = _bf(kv[:, (MLA_HEADS + h) * MLA_NOPE:(MLA_HEADS + h + 1) * MLA_NOPE])
        v_ref[:, b + MLA_V:b + 2 * MLA_V] = ones


def _flash_kernel(q_ref, k_ref, v_ref, o_ref, m_sc, acc_sc, s_sc, *, tk):
    qi = pl.program_id(2)
    m_sc[...] = jnp.full(m_sc.shape, -jnp.inf, F32)
    acc_sc[...] = jnp.zeros(acc_sc.shape, F32)

    def kv_rows(j):
        return pl.ds(pl.multiple_of(j * tk, tk), tk)

    def scores(sub, j):
        return _dot_nt(q_ref[sub * tk:(sub + 1) * tk, :], k_ref[kv_rows(j), :])

    def accumulate(sub, j, s, masked):
        rows = slice(sub * tk, (sub + 1) * tk)
        if masked:
            row = lax.broadcasted_iota(jnp.int32, s.shape, 0)
            col = lax.broadcasted_iota(jnp.int32, s.shape, 1)
            s = jnp.where(row >= col, s, -jnp.inf)
        m_prev = m_sc[rows, :]
        m_next = jnp.maximum(m_prev, jnp.max(s, axis=1, keepdims=True))
        alpha = jnp.exp2(m_prev - m_next)
        p = jnp.exp2(s - jnp.concatenate([m_next] * (tk // LANES), axis=1))
        acc_sc[rows, :] = (acc_sc[rows, :] * jnp.concatenate([alpha, alpha], axis=1)
                           + _dot(_bf(p), v_ref[kv_rows(j), :]))
        m_sc[rows, :] = m_next

    for sub in range(2):
        s_sc[0, sub] = scores(sub, 0)

    def body(jj, carry):
        j = 2 * jj
        for slot in range(2):
            for sub in range(2):
                s = s_sc[slot, sub]
                s_sc[1 - slot, sub] = scores(sub, j + slot + 1)
                accumulate(sub, j + slot, s, False)
        return carry

    half = lax.shift_right_logical(qi, 1)
    lax.fori_loop(0, half, lambda t, carry: body(2 * t + 1, body(2 * t, carry)), 0)
    lax.fori_loop(2 * half, qi, body, 0)
    j = 2 * qi
    s_last = scores(1, j + 1)
    accumulate(0, j, s_sc[0, 0], True)
    accumulate(1, j, s_sc[0, 1], False)
    accumulate(1, j + 1, s_last, True)
    acc = acc_sc[...]
    o_ref[...] = _bf(acc[:, :MLA_V] / acc[:, MLA_V:])


def _flash(q, k, v1, *, batch, seq, tk):
    tq = 2 * tk
    nq = seq // tq
    return pl.pallas_call(
        functools.partial(_flash_kernel, tk=tk),
        grid=(batch, MLA_HEADS, nq),
        in_specs=[pl.BlockSpec((tq, QK_HEAD_W), lambda b, h, i: (b * nq + i, h)),
                  pl.BlockSpec((seq, QK_HEAD_W), lambda b, h, i: (b, h)),
                  pl.BlockSpec((seq, 2 * MLA_V), lambda b, h, i: (b, h))],
        out_specs=pl.BlockSpec((tq, MLA_V), lambda b, h, i: (b * nq + i, h)),
        out_shape=jax.ShapeDtypeStruct((batch * seq, MLA_WIDTH), BF16),
        scratch_shapes=[pltpu.VMEM((tq, LANES), F32),
                        pltpu.VMEM((tq, 2 * MLA_V), F32),
                        pltpu.VMEM((2, 2, tk, tk), F32)],
        compiler_params=_params("parallel", "parallel", "arbitrary"),
        name="flash",
    )(q, k, v1)


def _np_masks():
    c = CHUNK
    r4 = np.arange(4 * c)
    tri = (np.arange(c)[:, None] >= np.arange(c)[None, :])
    m = {
        "tri": tri,
        "head64": (r4[:, None] // c) == (np.arange(RWKV_WIDTH)[None, :] // RWKV_N),
        "head32": (r4[:, None] // c) == (np.arange(GLA_QK_WIDTH)[None, :] // GLA_DK),
        "strict": np.arange(c)[:, None] > (r4[None, :] % c),
        "causal": np.arange(c)[:, None] >= (r4[None, :] % c),
    }
    return m


def _mask_inputs():
    m = _np_masks()
    return {
        "tri": jnp.asarray(m["tri"], BF16),
        "head64": jnp.asarray(m["head64"], F32),
        "head32": jnp.asarray(m["head32"], F32),
        "strict": jnp.asarray(m["strict"], F32),
        "causal": jnp.asarray(m["causal"], F32),
        "group64": jnp.asarray(m["head64"], BF16),
    }


def _stack(x, head_mask):
    return jnp.concatenate([x] * 4, axis=0) * head_mask


def _gla_kernel(qk_ref, v_ref, glo_ref, wgk_ref, bgk_ref, tri_ref, h32_ref, h64_ref, causal_ref,
                o_ref, st_sc, *, n_chunks):
    @pl.when(pl.program_id(0) == 0)
    def _():
        st_sc[...] = jnp.zeros(st_sc.shape, F32)

    c = CHUNK
    tri = tri_ref[...]
    h32 = h32_ref[...]
    h64 = h64_ref[...]
    causal = causal_ref[...] != 0
    nb = qk_ref.shape[0]
    items = [(b, slice(j * c, (j + 1) * c)) for j in range(n_chunks) for b in range(nb)]
    pre = [_dot(glo_ref[b], wgk_ref[...]) + bgk_ref[...] for b in range(nb)]
    gk = [-_softplus(-pre[b][rows, :]) / GLA_GATE_NORM for b, rows in items]
    q = [qk_ref[b, rows, :GLA_QK_WIDTH].astype(F32) for b, rows in items]
    k = [qk_ref[b, rows, GLA_QK_WIDTH:].astype(F32) for b, rows in items]
    v = [v_ref[b, rows, :] for b, rows in items]
    bc = _each(lambda x: _exact_left(tri, x), gk)
    b_last = _each(lambda x: x[c - 1:c, :], bc)
    q_t = _each(lambda q_, b_: q_ * (GLA_DK ** -0.5) * jnp.exp(b_), q, bc)
    k_t = _each(lambda k_, b_: k_ * jnp.exp(-b_), k, bc)
    k_end = _each(lambda k_, bl, b_: k_ * jnp.exp(bl - b_), k, b_last, bc)
    a = _each(lambda q_, k_: jnp.where(causal, _dot_nt(_bf(q_), _bf(_stack(k_, h32))), 0.0), q_t, k_t)
    o_intra = _each(lambda a_, v_: _dot(_bf(a_), _bf(_stack(v_, h64))), a, v)
    upd = _each(lambda v_, ke: _dot_tn(_bf(v_), _bf(ke)) * h32, v, k_end)
    decay = _each(jnp.exp, b_last)
    st = [st_sc[b] for b in range(nb)]
    for i, (b, rows) in enumerate(items):
        o_ref[b, rows, :] = o_intra[i] + _dot_nt(_bf(q_t[i]), _bf(st[b]))
        st[b] = st[b] * decay[i] + upd[i]
    for b in range(nb):
        st_sc[b] = st[b]


def _rwkv_prep_body(xc, last_prev, mu_ref, w0_ref, w2_ref, a0_ref, a2_ref, g2_ref, kk_ref, ka_ref,
                    rk_ref, grp_ref, r_out, lw_out, k_out, v_out, kk_out, b_out, g_out, bonus_out):
    row = lax.broadcasted_iota(jnp.int32, xc.shape, 0)
    prev = jnp.where(row == 0, last_prev, pltpu.roll(xc, 1, 0))
    xm = xc + (prev - xc) * mu_ref[...]
    w3 = RWKV_WIDTH
    r = xm[:, 0:w3]
    k = xm[:, w3:2 * w3]
    v = xm[:, 2 * w3:3 * w3]
    wa = xm[:, 3 * w3:3 * w3 + LANES]
    g_lo = xm[:, 3 * w3 + LANES:]
    w = -_softplus(-(w0_ref[...] + _dot(_bf(jnp.tanh(wa)), w2_ref[...]))) - 0.5
    a = _sigmoid(a0_ref[...] + _dot(_bf(wa), a2_ref[...]))
    g = _dot(_bf(_sigmoid(g_lo)), g2_ref[...])
    grp = grp_ref[...]
    kk = k * kk_ref[...]
    norm = jnp.sqrt(_exact_right(kk * kk, grp))
    kk = kk / jnp.maximum(norm, 1e-12)
    k2 = k * (1.0 + (a - 1.0) * ka_ref[...])
    r_out[...] = _bf(r)
    lw_out[...] = -jnp.exp(w)
    k_out[...] = _bf(k2)
    v_out[...] = _bf(v)
    kk_out[...] = _bf(kk)
    b_out[...] = _bf(kk * a)
    g_out[...] = _bf(g)
    bonus_out[...] = _bf(_exact_right(r * k2 * rk_ref[...], grp) * v)


def _each(fn, *lists):
    return [fn(*xs) for xs in zip(*lists)]


def _rwkv_pre(r, lw, k2, v, kk, bvec, tri, h64, strict, causal):
    c = CHUNK
    g = _each(lambda x: _exact_left(tri, x), lw)
    g_last = _each(lambda x: x[c - 1:c, :], g)
    e_neg = _each(lambda x: jnp.exp(-x), g)
    a_t = _each(lambda kk_, g_, lw_: -kk_ * jnp.exp(g_ - lw_), kk, g, lw)
    r_t = _each(lambda r_, g_: r_ * jnp.exp(g_), r, g)
    b_t = _each(jnp.multiply, bvec, e_neg)
    k_t = _each(jnp.multiply, k2, e_neg)
    e_end = _each(lambda gl, g_: jnp.exp(gl - g_), g_last, g)
    b_end = _each(jnp.multiply, bvec, e_end)
    k_end = _each(jnp.multiply, k2, e_end)

    h64b = _bf(h64)
    bd = lambda x: _stack(_bf(x), h64b)
    a_s = _each(bd, a_t)
    bk_s = _each(lambda b_, k_: jnp.concatenate([bd(b_), bd(k_)], axis=0), b_t, k_t)
    v_s = _each(bd, v)
    ar = _each(lambda a_, r_: _bf(jnp.concatenate([a_, r_], axis=0)), a_t, r_t)
    prod = _each(_dot_nt, ar, bk_s)
    l1 = _each(lambda x: jnp.where(strict, x[:c, :4 * c], 0.0), prod)
    a_ak = _each(lambda x: jnp.where(strict, x[:c, 4 * c:], 0.0), prod)
    a_rb = _each(lambda x: jnp.where(causal, x[c:, :4 * c], 0.0), prod)
    a_rk = _each(lambda x: jnp.where(causal, x[c:, 4 * c:], 0.0), prod)

    shape = l1[0].shape
    eye = (lax.broadcasted_iota(jnp.int32, shape, 0)
           == lax.broadcasted_iota(jnp.int32, shape, 1) % c).astype(F32)
    t_inv = _each(lambda x: eye + x, l1)
    lp = l1
    lp_bd = _each(bd, lp)
    for _ in range(5):
        lp = _each(lambda x, xb: _dot(_bf(x), xb), lp, lp_bd)
        lp_bd = _each(bd, lp)
        t_inv = _each(lambda t, pb: t + _dot(_bf(t), pb), t_inv, lp_bd)
    t_b = _each(_bf, t_inv)
    av_yv = _each(lambda a, ark, vs: _dot(_bf(jnp.concatenate([a, ark], axis=0)), vs), a_ak, a_rk, v_s)
    y_v = _each(lambda x: x[c:], av_yv)
    w_a = _each(_dot, t_b, a_s)
    u_v = _each(lambda t, x: _dot(t, bd(x[:c])), t_b, av_yv)

    bk_end = _each(lambda b_, k_: _bf(jnp.concatenate([b_, k_], axis=0)), b_end, k_end)
    decay = _each(jnp.exp, g_last)
    wr = _each(lambda w, r_: _bf(jnp.concatenate([w, r_], axis=0)), w_a, r_t)
    return {"wr": wr, "u_v": u_v, "a_rb": _each(_bf, a_rb), "y_v": y_v, "v": v, "bk_end": bk_end, "decay": decay}


def _rwkv_post(p, items, s, h64):
    c = CHUNK
    pick = lambda name: [p[name][i] for i in items]
    ur = _each(lambda wr, s_: _dot_nt(wr, _bf(s_)), pick("wr"), s)
    u = _each(lambda x, uv_: x[:c] + uv_, ur, pick("u_v"))
    h64b = _bf(h64)
    y = _each(lambda x, arb, u_, yv: x[c:] + _dot(arb, _stack(_bf(u_), h64b)) + yv,
              ur, pick("a_rb"), u, pick("y_v"))
    uv = _each(lambda u_, v_: _bf(jnp.concatenate([u_, v_], axis=0)), u, pick("v"))
    s_new = _each(lambda s_, d, x, z: s_ * d + _dot_tn(x, z) * h64, s, pick("decay"), uv, pick("bk_end"))
    return y, s_new


def _rwkv_kernel(r_ref, lw_ref, k_ref, v_ref, kk_ref, b_ref, tri_ref, h64_ref, strict_ref, causal_ref,
                 y_ref, s_sc, *, n_chunks):
    @pl.when(pl.program_id(0) == 0)
    def _():
        s_sc[...] = jnp.zeros(s_sc.shape, F32)

    tri = tri_ref[...]
    h64 = h64_ref[...]
    strict = strict_ref[...] != 0
    causal = causal_ref[...] != 0
    nb = r_ref.shape[0]
    c = CHUNK
    where = [(b, slice(j * c, (j + 1) * c)) for j in range(n_chunks) for b in range(nb)]
    load = lambda ref: [ref[b, rows, :].astype(F32) for b, rows in where]
    p = _rwkv_pre(load(r_ref), load(lw_ref), load(k_ref), load(v_ref), load(kk_ref), load(b_ref),
                  tri, h64, strict, causal)
    s = [s_sc[b] for b in range(nb)]
    for j in range(n_chunks):
        items = list(range(j * nb, (j + 1) * nb))
        y, s = _rwkv_post(p, items, s, h64)
        for b in range(nb):
            y_ref[b, j * c:(j + 1) * c, :] = y[b]
    for b in range(nb):
        s_sc[b] = s[b]


def _recur_kernel(qk_ref, gv_ref, glo_ref, wgk_ref, bgk_ref, r_ref, lw_ref, k_ref, v_ref, kk_ref, b_ref,
                  tri_ref, h32_ref, h64_ref, strict_ref, causal_ref, og_ref, y_ref, st_sc, s_sc, *, n_chunks):
    _rwkv_kernel(r_ref, lw_ref, k_ref, v_ref, kk_ref, b_ref, tri_ref, h64_ref, strict_ref, causal_ref,
                 y_ref, s_sc, n_chunks=n_chunks)
    _gla_kernel(qk_ref, gv_ref, glo_ref, wgk_ref, bgk_ref, tri_ref, h32_ref, h64_ref, causal_ref,
                og_ref, st_sc, n_chunks=n_chunks)


def _recur(pb, wgk, bgk, r, lw, k2, v, kk, bvec, masks, *, batch, seq, tc):
    t = pb.shape[0]
    const = lambda s: (0, 0)
    pb3 = pb.reshape(batch, seq, pb.shape[1])
    seqs = [a.reshape(batch, seq, RWKV_WIDTH) for a in (r, lw, k2, v, kk, bvec)]
    spec = pl.BlockSpec((batch, tc, RWKV_WIDTH), lambda s: (0, s, 0))
    mask_names = ("tri", "head32", "head64", "strict", "causal")
    og, y = pl.pallas_call(
        functools.partial(_recur_kernel, n_chunks=tc // CHUNK),
        grid=(seq // tc,),
        in_specs=[pl.BlockSpec((batch, tc, 2 * GLA_QK_WIDTH), lambda s: (0, s, 0)),
                  pl.BlockSpec((batch, tc, GLA_WIDTH), lambda s: (0, s, 1)),
                  pl.BlockSpec((batch, tc, LANES), lambda s: (0, s, 6)),
                  pl.BlockSpec(wgk.shape, const),
                  pl.BlockSpec(bgk.shape, const)]
                 + [spec] * 6 + [pl.BlockSpec(masks[n].shape, const) for n in mask_names],
        out_specs=[pl.BlockSpec((batch, tc, GLA_WIDTH), lambda s: (0, s, 0)), spec],
        out_shape=[jax.ShapeDtypeStruct((batch, seq, GLA_WIDTH), F32),
                   jax.ShapeDtypeStruct((batch, seq, RWKV_WIDTH), F32)],
        scratch_shapes=[pltpu.VMEM((batch, GLA_WIDTH, GLA_QK_WIDTH), F32),
                        pltpu.VMEM((batch, RWKV_WIDTH, RWKV_WIDTH), F32)],
        compiler_params=_params("arbitrary"),
        name="recur",
    )(pb3, pb3, pb3, wgk, bgk, *seqs, *[masks[n] for n in mask_names])
    return og.reshape(t, GLA_WIDTH), y.reshape(t, RWKV_WIDTH)


def _mix_out_body(x, oa_ref, ob_ref, gout_ref, yc_ref, bonus_ref, gate_ref, na_ref, nb_ref,
                  lng_ref, lnb_ref, grp_ref, wa_ref, wb_ref, wc_ref):
    grp = grp_ref[...]
    ya = _rms(oa_ref[...].astype(F32), na_ref[...])
    ob = ob_ref[...]
    ms = _exact_right(ob * ob, grp) * (1.0 / GLA_DV)
    gout = gout_ref[...].astype(F32)
    yb = ob * lax.rsqrt(ms + NORM_EPS) * nb_ref[...] * (gout * _sigmoid(gout))
    y = yc_ref[...]
    mean = _exact_right(y, grp) * (1.0 / RWKV_N)
    d = y - mean
    var = _exact_right(d * d, grp) * (1.0 / RWKV_N)
    yc = (d * lax.rsqrt(var + RWKV_LN_EPS) * lng_ref[...] + lnb_ref[...] + bonus_ref[...].astype(F32)) * gate_ref[...].astype(F32)
    return x + _dot(_bf(ya), wa_ref[...]) + _dot(_bf(yb), wb_ref[...]) + _dot(_bf(yc), wc_ref[...])


def _shift_rows(y, carry):
    return jnp.concatenate([carry[SUBLANES - 1:SUBLANES, :], y[:y.shape[0] - 1, :]], axis=0)


def _causal_conv3(u, prev, cw, cb):
    c0, c1, c2 = cw[0:1, :], cw[1:2, :], cw[2:3, :]
    z_prev = c0 * prev
    w_prev = c1 * prev + pltpu.roll(z_prev, 1, 0)
    w = c1 * u + _shift_rows(c0 * u, z_prev)
    return c2 * u + cb + _shift_rows(w, w_prev)


def _mix_ffn_kernel(x_ref, oa_ref, ob_ref, gout_ref, yc_ref, bonus_ref, gate_ref, na_ref, nb_ref, lng_ref, lnb_ref,
                    grp_ref, wa_ref, wb_ref, wc_ref, g_ref, wu_ref, cw_ref, cb_ref, wd_ref, fg_ref,
                    o_ref, cg_sc, cv_sc, *, d_ff, tf, final_norm):
    x = _mix_out_body(x_ref[...], oa_ref, ob_ref, gout_ref, yc_ref, bonus_ref, gate_ref, na_ref, nb_ref,
                      lng_ref, lnb_ref, grp_ref, wa_ref, wb_ref, wc_ref)
    h = _bf(_rms(x, g_ref[...]))
    ts = h.shape[0]
    first = pl.program_id(1) == 0
    n_f = d_ff // tf

    def up(c):
        return (_dot(h, wu_ref[:, c * tf:(c + 1) * tf]),
                _dot(h, wu_ref[:, d_ff + c * tf:d_ff + (c + 1) * tf]))

    acc = None
    nxt = up(0)
    for c in range(n_f):
        ug, uv = nxt
        if c + 1 < n_f:
            nxt = up(c + 1)
        gcol = slice(c * tf, (c + 1) * tf)
        vcol = slice(d_ff + c * tf, d_ff + (c + 1) * tf)
        gate = _causal_conv3(ug, jnp.where(first, 0.0, cg_sc[c]), cw_ref[:, gcol], cb_ref[:, gcol])
        val = _causal_conv3(uv, jnp.where(first, 0.0, cv_sc[c]), cw_ref[:, vcol], cb_ref[:, vcol])
        cg_sc[c] = ug[ts - SUBLANES:, :]
        cv_sc[c] = uv[ts - SUBLANES:, :]
        part = _dot(_bf(gate * _sigmoid(gate) * val), wd_ref[gcol, :])
        acc = part if acc is None else acc + part
    y = x + acc
    if final_norm:
        y = _rms(y, fg_ref[...])
    o_ref[...] = y


def _mix_ffn(x, mixers, mix_params, g, w_up, conv_w, conv_b, w_down, final_g, *, batch, seq, ts, tf, final_norm):
    t, d = x.shape
    d_ff = w_down.shape[0]
    nsb = seq // ts
    const = lambda b, s: (0, 0)
    rows = lambda width, col=0: pl.BlockSpec((ts, width), lambda b, s: (b * nsb + s, col))
    resident = lambda a: pl.BlockSpec(a.shape, const, pipeline_mode=pl.Buffered(1))
    return pl.pallas_call(
        functools.partial(_mix_ffn_kernel, d_ff=d_ff, tf=tf, final_norm=final_norm),
        grid=(batch, nsb),
        in_specs=[rows(d), rows(MLA_WIDTH), rows(GLA_WIDTH),
                  rows(GLA_WIDTH, 2),
                  rows(RWKV_WIDTH), rows(RWKV_WIDTH), rows(RWKV_WIDTH)]
                 + [resident(a) for a in mix_params]
                 + [pl.BlockSpec((1, d), const),
                    resident(w_up), resident(conv_w), resident(conv_b), resident(w_down),
                    pl.BlockSpec((1, d), const)],
        out_specs=rows(d),
        out_shape=jax.ShapeDtypeStruct((t, d), F32),
        scratch_shapes=[pltpu.VMEM((d_ff // tf, SUBLANES, tf), F32),
                        pltpu.VMEM((d_ff // tf, SUBLANES, tf), F32)],
        compiler_params=_params("parallel", "arbitrary"),
        name="mix_ffn",
    )(x, *mixers, *mix_params, g, w_up, conv_w, conv_b, w_down, final_g)


def _rot_cols(w):
    half = w.shape[-1] // 2
    return jnp.concatenate([-w[..., half:], w[..., :half]], axis=-1)


def _regroup_w_in(w_in):
    d = w_in.shape[0]
    a0 = 0
    b0 = MLA_COLS
    c0 = MLA_COLS + GLA_COLS
    k_pe = w_in[:, MLA_Q_LORA + MLA_KV_LORA:MLA_COLS]
    gq = w_in[:, b0:b0 + GLA_QK_WIDTH]
    gk = w_in[:, b0 + GLA_QK_WIDTH:b0 + 2 * GLA_QK_WIDTH]
    gv = w_in[:, b0 + 2 * GLA_QK_WIDTH:b0 + 2 * GLA_QK_WIDTH + GLA_WIDTH]
    glo = w_in[:, b0 + 2 * GLA_QK_WIDTH + GLA_WIDTH:b0 + 2 * GLA_QK_WIDTH + GLA_WIDTH + GLA_GATE_RANK]
    gout = w_in[:, b0 + 2 * GLA_QK_WIDTH + GLA_WIDTH + GLA_GATE_RANK:c0]
    pad = jnp.zeros((d, LANES - GLA_GATE_RANK), w_in.dtype)
    cols = [w_in[:, a0:MLA_COLS], _rot_cols(k_pe), gq, gk, gv, gout, glo, pad, w_in[:, c0:]]
    return _bf(jnp.concatenate(cols, axis=1))


def _regroup_w_uq(w_uq):
    r = w_uq.shape[0]
    w = w_uq.reshape(r, MLA_HEADS, MLA_QK)
    nope = w[:, :, :MLA_NOPE]
    pe = w[:, :, MLA_NOPE:]
    return _bf(jnp.concatenate([nope, pe, _rot_cols(pe)], axis=-1).reshape(r, MLA_HEADS * QK_HEAD_W))


def _regroup_w_ukv(w_ukv):
    r = w_ukv.shape[0]
    w = w_ukv.reshape(r, MLA_HEADS, MLA_NOPE + MLA_V)
    k_nope = w[:, :, :MLA_NOPE].reshape(r, MLA_HEADS * MLA_NOPE)
    v = w[:, :, MLA_NOPE:].reshape(r, MLA_HEADS * MLA_V)
    return _bf(jnp.concatenate([k_nope, v], axis=1))


def _rope_table(seq):
    inv = 1.0 / (ROPE_THETA ** (jnp.arange(0, MLA_ROPE, 2, dtype=F32) / MLA_ROPE))
    ang = jnp.arange(seq, dtype=F32)[:, None] * inv[None, :]
    cos, sin = jnp.cos(ang), jnp.sin(ang)
    return jnp.concatenate([cos, cos, sin, sin], axis=1)


def _row(v):
    return v.reshape(1, -1).astype(F32)


def _pad_rows(w, top, total):
    return jnp.concatenate([jnp.zeros((top, w.shape[1]), w.dtype), w,
                            jnp.zeros((total - top - w.shape[0], w.shape[1]), w.dtype)], axis=0)


def _tiles(seq):
    return {
        "tm": min(512, seq),
        "tk": min(512, seq // 2),
        "tc": min(256, seq),
        "ts": min(512, seq),
        "tf": 256,
    }


def kernel(x, ln1_g, w_in, mla_q_norm_g, mla_w_uq, mla_kv_norm_g, mla_w_ukv, mla_out_norm_g, gla_w_gk, gla_b_gk, gla_norm_g, rwkv_mu, rwkv_w0, rwkv_w2, rwkv_a0, rwkv_a2, rwkv_g2, rwkv_k_k, rwkv_k_a, rwkv_r_k, rwkv_ln_g, rwkv_ln_b, w_out, ln2_g, ffn_w_up, ffn_conv_w, ffn_conv_b, ffn_w_down, final_g):
    batch, seq, d_model = x.shape
    depth = w_in.shape[0]
    tl = _tiles(seq)
    masks = _mask_inputs()
    grp = masks["group64"]
    rope = _rope_table(seq)
    xt = x.reshape(batch * seq, d_model)
    for l in range(depth):
        w2p = _bf(_pad_rows(rwkv_w2[l], 0, LANES))
        a2p = _bf(_pad_rows(rwkv_a2[l], RWKV_DECAY_RANK, LANES))
        mla = (_row(mla_q_norm_g[l]), _row(mla_kv_norm_g[l]), _regroup_w_uq(mla_w_uq[l]),
               _regroup_w_ukv(mla_w_ukv[l]))
        rwkv = (_row(rwkv_mu[l]), _row(rwkv_w0[l]), w2p, _row(rwkv_a0[l]), a2p, _bf(rwkv_g2[l]),
                _row(rwkv_k_k[l]), _row(rwkv_k_a[l]), _row(rwkv_r_k[l]))
        q, k, v, pb, r, lw, k2, vv, kk, bvec, gate, bonus = _proj(
            xt, _row(ln1_g[l]), _regroup_w_in(w_in[l]), mla, rwkv, rope, grp, tm=tl["tm"], seq=seq)
        o_mla = _flash(q, k, v, batch=batch, seq=seq, tk=tl["tk"])
        wgk = _bf(_pad_rows(gla_w_gk[l], 0, LANES))
        o_gla, y_rwkv = _recur(pb, wgk, _row(gla_b_gk[l]), r, lw, k2, vv, kk, bvec, masks,
                               batch=batch, seq=seq, tc=tl["tc"])
        wo = _bf(w_out[l])
        mix_params = (_row(mla_out_norm_g[l]), _row(jnp.tile(gla_norm_g[l], GLA_HEADS)), _row(rwkv_ln_g[l]),
                      _row(rwkv_ln_b[l]), grp, wo[:MLA_WIDTH], wo[MLA_WIDTH:MLA_WIDTH + GLA_WIDTH],
                      wo[MLA_WIDTH + GLA_WIDTH:])
        xt = _mix_ffn(xt, (o_mla, o_gla, pb, y_rwkv, bonus, gate), mix_params, _row(ln2_g[l]), _bf(ffn_w_up[l]),
                      ffn_conv_w[l].astype(F32), _row(ffn_conv_b[l]), _bf(ffn_w_down[l]), _row(final_g),
                      batch=batch, seq=seq, ts=tl["ts"], tf=tl["tf"], final_norm=(l == depth - 1))
    return xt.reshape(batch, seq, d_model)
```

```python
import functools
import math

import numpy as np
import jax
import jax.numpy as jnp
from jax import lax
from jax.experimental import pallas as pl
from jax.experimental.pallas import tpu as pltpu

F32 = jnp.float32
BF16 = jnp.bfloat16

MLA_HEADS = 4
MLA_NOPE = 128
MLA_ROPE = 64
MLA_V = 128
MLA_QK = MLA_NOPE + MLA_ROPE
MLA_Q_LORA = 384
MLA_KV_LORA = 256
MLA_WIDTH = MLA_HEADS * MLA_V
ROPE_THETA = 10000.0
GLA_HEADS = 4
GLA_DK = 32
GLA_DV = 64
GLA_QK_WIDTH = GLA_HEADS * GLA_DK
GLA_WIDTH = GLA_HEADS * GLA_DV
GLA_GATE_RANK = 16
GLA_GATE_NORM = 16.0
RWKV_HEADS = 4
RWKV_N = 64
RWKV_WIDTH = RWKV_HEADS * RWKV_N
RWKV_DECAY_RANK = 64
RWKV_A_RANK = 64
RWKV_GATE_RANK = 128
RWKV_LN_EPS = 64e-5
MLA_COLS = MLA_Q_LORA + MLA_KV_LORA + MLA_ROPE
GLA_COLS = 2 * GLA_QK_WIDTH + GLA_WIDTH + GLA_GATE_RANK + GLA_WIDTH
RWKV_COLS = 3 * RWKV_WIDTH + RWKV_DECAY_RANK + RWKV_A_RANK + RWKV_GATE_RANK
NORM_EPS = 1e-6
CONV_WIDTH = 3

LANES = 128
SUBLANES = 8
VMEM_LIMIT = 56 * 1024 * 1024

PA_W = MLA_Q_LORA + MLA_KV_LORA + 2 * MLA_ROPE
PB_W = 2 * GLA_QK_WIDTH + 2 * GLA_WIDTH + LANES
PC_W = RWKV_COLS
QK_HEAD_W = 2 * LANES

LOG2E = math.log2(math.e)

CHUNK = 64


def _dot(a, b):
    return lax.dot_general(a, b, (((1,), (0,)), ((), ())), preferred_element_type=F32)


def _dot_nt(a, b):
    return lax.dot_general(a, b, (((1,), (1,)), ((), ())), preferred_element_type=F32)


def _dot_tn(a, b):
    return lax.dot_general(a, b, (((0,), (0,)), ((), ())), preferred_element_type=F32)


def _bf(x):
    return x.astype(BF16)


def _split_terms(x, n):
    terms = []
    rem = x
    for _ in range(n):
        t = rem.astype(BF16)
        terms.append(t)
        rem = rem - t.astype(F32)
    return terms


def _exact_left(m, x, n=3):
    out = None
    for t in _split_terms(x, n):
        y = _dot(m, t)
        out = y if out is None else out + y
    return out


def _exact_right(x, m, n=2):
    out = None
    for t in _split_terms(x, n):
        y = _dot(t, m)
        out = y if out is None else out + y
    return out


def _rms(x, g, eps=NORM_EPS):
    return x * lax.rsqrt(jnp.mean(x * x, axis=-1, keepdims=True) + eps) * g


def _sigmoid(x):
    return 1.0 / (1.0 + jnp.exp(-x))


def _softplus(x):
    return jnp.maximum(x, 0.0) + jnp.log(1.0 + jnp.exp(-jnp.abs(x)))


def _params(*sem):
    return pltpu.CompilerParams(dimension_semantics=sem, vmem_limit_bytes=VMEM_LIMIT)


def _proj_kernel(x_ref, g_ref, w_ref, gq_ref, gkv_ref, wq_ref, wkv_ref, rope_ref, mu_ref, w0_ref, w2_ref,
                 a0_ref, a2_ref, g2_ref, kk_ref, ka_ref, rk_ref, grp_ref,
                 q_out, k_out, v_out, pb_out, r_out, lw_out, kr_out, vr_out, kkr_out, b_out, gate_out, bonus_out,
                 carry_sc, *, blocks_per_seq, n_step):
    h = _bf(_rms(x_ref[...], g_ref[...]))

    def group(base, width):
        cols = [_dot(h, w_ref[:, base + c:base + min(c + n_step, width)]) for c in range(0, width, n_step)]
        return cols[0] if len(cols) == 1 else jnp.concatenate(cols, axis=1)

    pa = group(0, PA_W)
    pc = group(PA_W + PB_W, PC_W)
    pb_out[...] = _bf(group(PA_W, PB_W))
    _mla_prep_body(pa, gq_ref, gkv_ref, wq_ref, wkv_ref, rope_ref, q_out, k_out, v_out)
    first = (pl.program_id(0) % blocks_per_seq) == 0
    last_prev = jnp.where(first, 0.0, carry_sc[SUBLANES - 1:SUBLANES, :])
    carry_sc[...] = pc[pc.shape[0] - SUBLANES:, :]
    _rwkv_prep_body(pc, last_prev, mu_ref, w0_ref, w2_ref, a0_ref, a2_ref, g2_ref, kk_ref, ka_ref, rk_ref, grp_ref,
                    r_out, lw_out, kr_out, vr_out, kkr_out, b_out, gate_out, bonus_out)


def _proj(x, g, w, mla, rwkv, rope, grp, *, tm, seq):
    t, d = x.shape
    const = lambda i: (0, 0)
    resident = lambda a: pl.BlockSpec(a.shape, const, pipeline_mode=pl.Buffered(1))
    row = lambda width: pl.BlockSpec((tm, width), lambda i: (i, 0))
    sds = lambda width, dt: jax.ShapeDtypeStruct((t, width), dt)
    qk_w = MLA_HEADS * QK_HEAD_W
    n_seq_blocks = seq // tm
    return pl.pallas_call(
        functools.partial(_proj_kernel, blocks_per_seq=n_seq_blocks, n_step=512),
        grid=(t // tm,),
        in_specs=[row(d), pl.BlockSpec((1, d), const), resident(w)]
                 + [resident(a) for a in mla]
                 + [pl.BlockSpec((tm, LANES), lambda i: (i % n_seq_blocks, 0))]
                 + [resident(a) for a in rwkv] + [resident(grp)],
        out_specs=[row(qk_w), row(qk_w), row(2 * MLA_WIDTH), row(PB_W)] + [row(RWKV_WIDTH)] * 8,
        out_shape=[sds(qk_w, BF16), sds(qk_w, BF16), sds(2 * MLA_WIDTH, BF16), sds(PB_W, BF16),
                   sds(RWKV_WIDTH, BF16), sds(RWKV_WIDTH, F32)] + [sds(RWKV_WIDTH, BF16)] * 6,
        scratch_shapes=[pltpu.VMEM((SUBLANES, PC_W), F32)],
        compiler_params=_params("arbitrary"),
        name="proj",
    )(x, g, w, *mla, rope, *rwkv, grp)


def _mla_prep_body(pa, gq_ref, gkv_ref, wq_ref, wkv_ref, rope_ref, q_ref, k_ref, v_ref):
    rope = rope_ref[...]
    cq = _bf(_rms(pa[:, :MLA_Q_LORA], gq_ref[...]))
    q = _dot(cq, wq_ref[...]) * (MLA_QK ** -0.5 * LOG2E)
    for h in range(MLA_HEADS):
        b = h * QK_HEAD_W
        q_ref[:, b:b + LANES] = _bf(q[:, b:b + LANES])
        q_ref[:, b + LANES:b + QK_HEAD_W] = _bf(q[:, b + LANES:b + QK_HEAD_W] * rope)
    ckv = _bf(_rms(pa[:, MLA_Q_LORA:MLA_Q_LORA + MLA_KV_LORA], gkv_ref[...]))
    kv = _dot(ckv, wkv_ref[...])
    kr = pa[:, MLA_Q_LORA + MLA_KV_LORA:] * rope
    kr = _bf(kr + pltpu.roll(kr, MLA_ROPE, 1))
    for h in range(MLA_HEADS):
        b = h * QK_HEAD_W
        k_ref[:, b:b + LANES] = _bf(kv[:, h * MLA_NOPE:(h + 1) * MLA_NOPE])
        k_ref[:, b + LANES:b + QK_HEAD_W] = kr
    ones = jnp.ones((pa.shape[0], MLA_V), BF16)
    for h in range(MLA_HEADS):
        b = h * 2 * MLA_V
        v_ref[:, b:b + MLA_V] = _bf(kv[:, (MLA_HEADS + h) * MLA_NOPE:(MLA_HEADS + h + 1) * MLA_NOPE])
        v_ref[:, b + MLA_V:b + 2 * MLA_V] = ones


def _flash_kernel(q_ref, k_ref, v_ref, o_ref, m_sc, acc_sc, s_sc, *, tk):
    qi = pl.program_id(2)
    m_sc[...] = jnp.full(m_sc.shape, -jnp.inf, F32)
    acc_sc[...] = jnp.zeros(acc_sc.shape, F32)

    def kv_rows(j):
        return pl.ds(pl.multiple_of(j * tk, tk), tk)

    def scores(sub, j):
        return _dot_nt(q_ref[sub * tk:(sub + 1) * tk, :], k_ref[kv_rows(j), :])

    def accumulate(sub, j, s, masked):
        rows = slice(sub * tk, (sub + 1) * tk)
        if masked:
            row = lax.broadcasted_iota(jnp.int32, s.shape, 0)
            col = lax.broadcasted_iota(jnp.int32, s.shape, 1)
            s = jnp.where(row >= col, s, -jnp.inf)
        m_prev = m_sc[rows, :]
        m_next = jnp.maximum(m_prev, jnp.max(s, axis=1, keepdims=True))
        alpha = jnp.exp2(m_prev - m_next)
        p = jnp.exp2(s - jnp.concatenate([m_next] * (tk // LANES), axis=1))
        acc_sc[rows, :] = (acc_sc[rows, :] * jnp.concatenate([alpha, alpha], axis=1)
                           + _dot(_bf(p), v_ref[kv_rows(j), :]))
        m_sc[rows, :] = m_next

    for sub in range(2):
        s_sc[0, sub] = scores(sub, 0)

    def body(jj, carry):
        j = 2 * jj
        for slot in range(2):
            for sub in range(2):
                s = s_sc[slot, sub]
                s_sc[1 - slot, sub] = scores(sub, j + slot + 1)
                accumulate(sub, j + slot, s, False)
        return carry

    half = lax.shift_right_logical(qi, 1)
    lax.fori_loop(0, half, lambda t, carry: body(2 * t + 1, body(2 * t, carry)), 0)
    lax.fori_loop(2 * half, qi, body, 0)
    j = 2 * qi
    s_last = scores(1, j + 1)
    accumulate(0, j, s_sc[0, 0], True)
    accumulate(1, j, s_sc[0, 1], False)
    accumulate(1, j + 1, s_last, True)
    acc = acc_sc[...]
    o_ref[...] = _bf(acc[:, :MLA_V] / acc[:, MLA_V:])


def _flash(q, k, v1, *, batch, seq, tk):
    tq = 2 * tk
    nq = seq // tq
    return pl.pallas_call(
        functools.partial(_flash_kernel, tk=tk),
        grid=(batch, MLA_HEADS, nq),
        in_specs=[pl.BlockSpec((tq, QK_HEAD_W), lambda b, h, i: (b * nq + i, h)),
                  pl.BlockSpec((seq, QK_HEAD_W), lambda b, h, i: (b, h)),
                  pl.BlockSpec((seq, 2 * MLA_V), lambda b, h, i: (b, h))],
        out_specs=pl.BlockSpec((tq, MLA_V), lambda b, h, i: (b * nq + i, h)),
        out_shape=jax.ShapeDtypeStruct((batch * seq, MLA_WIDTH), BF16),
        scratch_shapes=[pltpu.VMEM((tq, LANES), F32),
                        pltpu.VMEM((tq, 2 * MLA_V), F32),
                        pltpu.VMEM((2, 2, tk, tk), F32)],
        compiler_params=_params("parallel", "parallel", "arbitrary"),
        name="flash",
    )(q, k, v1)


def _np_masks():
    c = CHUNK
    r4 = np.arange(4 * c)
    tri = (np.arange(c)[:, None] >= np.arange(c)[None, :])
    m = {
        "tri": tri,
        "head64": (r4[:, None] // c) == (np.arange(RWKV_WIDTH)[None, :] // RWKV_N),
        "head32": (r4[:, None] // c) == (np.arange(GLA_QK_WIDTH)[None, :] // GLA_DK),
        "strict": np.arange(c)[:, None] > (r4[None, :] % c),
        "causal": np.arange(c)[:, None] >= (r4[None, :] % c),
    }
    return m


def _mask_inputs():
    m = _np_masks()
    return {
        "tri": jnp.asarray(m["tri"], BF16),
        "head64": jnp.asarray(m["head64"], F32),
        "head32": jnp.asarray(m["head32"], F32),
        "strict": jnp.asarray(m["strict"], F32),
        "causal": jnp.asarray(m["causal"], F32),
        "group64": jnp.asarray(m["head64"], BF16),
    }


def _stack(x, head_mask):
    return jnp.concatenate([x] * 4, axis=0) * head_mask


def _gla_kernel(qk_ref, v_ref, glo_ref, wgk_ref, bgk_ref, tri_ref, h32_ref, h64_ref, causal_ref,
                o_ref, st_sc, *, n_chunks):
    @pl.when(pl.program_id(0) == 0)
    def _():
        st_sc[...] = jnp.zeros(st_sc.shape, F32)

    c = CHUNK
    tri = tri_ref[...]
    h32 = h32_ref[...]
    h64 = h64_ref[...]
    causal = causal_ref[...] != 0
    nb = qk_ref.shape[0]
    items = [(b, slice(j * c, (j + 1) * c)) for j in range(n_chunks) for b in range(nb)]
    pre = [_dot(glo_ref[b], wgk_ref[...]) + bgk_ref[...] for b in range(nb)]
    gk = [-_softplus(-pre[b][rows, :]) / GLA_GATE_NORM for b, rows in items]
    q = [qk_ref[b, rows, :GLA_QK_WIDTH].astype(F32) for b, rows in items]
    k = [qk_ref[b, rows, GLA_QK_WIDTH:].astype(F32) for b, rows in items]
    v = [v_ref[b, rows, :] for b, rows in items]
    bc = _each(lambda x: _exact_left(tri, x), gk)
    b_last = _each(lambda x: x[c - 1:c, :], bc)
    q_t = _each(lambda q_, b_: q_ * (GLA_DK ** -0.5) * jnp.exp(b_), q, bc)
    k_t = _each(lambda k_, b_: k_ * jnp.exp(-b_), k, bc)
    k_end = _each(lambda k_, bl, b_: k_ * jnp.exp(bl - b_), k, b_last, bc)
    a = _each(lambda q_, k_: jnp.where(causal, _dot_nt(_bf(q_), _bf(_stack(k_, h32))), 0.0), q_t, k_t)
    o_intra = _each(lambda a_, v_: _dot(_bf(a_), _bf(_stack(v_, h64))), a, v)
    upd = _each(lambda v_, ke: _dot_tn(_bf(v_), _bf(ke)) * h32, v, k_end)
    decay = _each(jnp.exp, b_last)
    st = [st_sc[b] for b in range(nb)]
    for i, (b, rows) in enumerate(items):
        o_ref[b, rows, :] = o_intra[i] + _dot_nt(_bf(q_t[i]), _bf(st[b]))
        st[b] = st[b] * decay[i] + upd[i]
    for b in range(nb):
        st_sc[b] = st[b]


def _rwkv_prep_body(xc, last_prev, mu_ref, w0_ref, w2_ref, a0_ref, a2_ref, g2_ref, kk_ref, ka_ref,
                    rk_ref, grp_ref, r_out, lw_out, k_out, v_out, kk_out, b_out, g_out, bonus_out):
    row = lax.broadcasted_iota(jnp.int32, xc.shape, 0)
    prev = jnp.where(row == 0, last_prev, pltpu.roll(xc, 1, 0))
    xm = xc + (prev - xc) * mu_ref[...]
    w3 = RWKV_WIDTH
    r = xm[:, 0:w3]
    k = xm[:, w3:2 * w3]
    v = xm[:, 2 * w3:3 * w3]
    wa = xm[:, 3 * w3:3 * w3 + LANES]
    g_lo = xm[:, 3 * w3 + LANES:]
    w = -_softplus(-(w0_ref[...] + _dot(_bf(jnp.tanh(wa)), w2_ref[...]))) - 0.5
    a = _sigmoid(a0_ref[...] + _dot(_bf(wa), a2_ref[...]))
    g = _dot(_bf(_sigmoid(g_lo)), g2_ref[...])
    grp = grp_ref[...]
    kk = k * kk_ref[...]
    norm = jnp.sqrt(_exact_right(kk * kk, grp))
    kk = kk / jnp.maximum(norm, 1e-12)
    k2 = k * (1.0 + (a - 1.0) * ka_ref[...])
    r_out[...] = _bf(r)
    lw_out[...] = -jnp.exp(w)
    k_out[...] = _bf(k2)
    v_out[...] = _bf(v)
    kk_out[...] = _bf(kk)
    b_out[...] = _bf(kk * a)
    g_out[...] = _bf(g)
    bonus_out[...] = _bf(_exact_right(r * k2 * rk_ref[...], grp) * v)


def _each(fn, *lists):
    return [fn(*xs) for xs in zip(*lists)]


def _rwkv_pre(r, lw, k2, v, kk, bvec, tri, h64, strict, causal):
    c = CHUNK
    g = _each(lambda x: _exact_left(tri, x), lw)
    g_last = _each(lambda x: x[c - 1:c, :], g)
    e_neg = _each(lambda x: jnp.exp(-x), g)
    a_t = _each(lambda kk_, g_, lw_: -kk_ * jnp.exp(g_ - lw_), kk, g, lw)
    r_t = _each(lambda r_, g_: r_ * jnp.exp(g_), r, g)
    b_t = _each(jnp.multiply, bvec, e_neg)
    k_t = _each(jnp.multiply, k2, e_neg)
    e_end = _each(lambda gl, g_: jnp.exp(gl - g_), g_last, g)
    b_end = _each(jnp.multiply, bvec, e_end)
    k_end = _each(jnp.multiply, k2, e_end)

    h64b = _bf(h64)
    bd = lambda x: _stack(_bf(x), h64b)
    a_s = _each(bd, a_t)
    bk_s = _each(lambda b_, k_: jnp.concatenate([bd(b_), bd(k_)], axis=0), b_t, k_t)
    v_s = _each(bd, v)
    ar = _each(lambda a_, r_: _bf(jnp.concatenate([a_, r_], axis=0)), a_t, r_t)
    prod = _each(_dot_nt, ar, bk_s)
    l1 = _each(lambda x: jnp.where(strict, x[:c, :4 * c], 0.0), prod)
    a_ak = _each(lambda x: jnp.where(strict, x[:c, 4 * c:], 0.0), prod)
    a_rb = _each(lambda x: jnp.where(causal, x[c:, :4 * c], 0.0), prod)
    a_rk = _each(lambda x: jnp.where(causal, x[c:, 4 * c:], 0.0), prod)

    shape = l1[0].shape
    eye = (lax.broadcasted_iota(jnp.int32, shape, 0)
           == lax.broadcasted_iota(jnp.int32, shape, 1) % c).astype(F32)
    t_inv = _each(lambda x: eye + x, l1)
    lp = l1
    lp_bd = _each(bd, lp)
    for _ in range(5):
        lp = _each(lambda x, xb: _dot(_bf(x), xb), lp, lp_bd)
        lp_bd = _each(bd, lp)
        t_inv = _each(lambda t, pb: t + _dot(_bf(t), pb), t_inv, lp_bd)
    t_b = _each(_bf, t_inv)
    av_yv = _each(lambda a, ark, vs: _dot(_bf(jnp.concatenate([a, ark], axis=0)), vs), a_ak, a_rk, v_s)
    y_v = _each(lambda x: x[c:], av_yv)
    w_a = _each(_dot, t_b, a_s)
    u_v = _each(lambda t, x: _dot(t, bd(x[:c])), t_b, av_yv)

    bk_end = _each(lambda b_, k_: _bf(jnp.concatenate([b_, k_], axis=0)), b_end, k_end)
    decay = _each(jnp.exp, g_last)
    wr = _each(lambda w, r_: _bf(jnp.concatenate([w, r_], axis=0)), w_a, r_t)
    return {"wr": wr, "u_v": u_v, "a_rb": _each(_bf, a_rb), "y_v": y_v, "v": v, "bk_end": bk_end, "decay": decay}


def _rwkv_post(p, items, s, h64):
    c = CHUNK
    pick = lambda name: [p[name][i] for i in items]
    ur = _each(lambda wr, s_: _dot_nt(wr, _bf(s_)), pick("wr"), s)
    u = _each(lambda x, uv_: x[:c] + uv_, ur, pick("u_v"))
    h64b = _bf(h64)
    y = _each(lambda x, arb, u_, yv: x[c:] + _dot(arb, _stack(_bf(u_), h64b)) + yv,
              ur, pick("a_rb"), u, pick("y_v"))
    uv = _each(lambda u_, v_: _bf(jnp.concatenate([u_, v_], axis=0)), u, pick("v"))
    s_new = _each(lambda s_, d, x, z: s_ * d + _dot_tn(x, z) * h64, s, pick("decay"), uv, pick("bk_end"))
    return y, s_new


def _rwkv_kernel(r_ref, lw_ref, k_ref, v_ref, kk_ref, b_ref, tri_ref, h64_ref, strict_ref, causal_ref,
                 y_ref, s_sc, *, n_chunks):
    @pl.when(pl.program_id(0) == 0)
    def _():
        s_sc[...] = jnp.zeros(s_sc.shape, F32)

    tri = tri_ref[...]
    h64 = h64_ref[...]
    strict = strict_ref[...] != 0
    causal = causal_ref[...] != 0
    nb = r_ref.shape[0]
    c = CHUNK
    where = [(b, slice(j * c, (j + 1) * c)) for j in range(n_chunks) for b in range(nb)]
    load = lambda ref: [ref[b, rows, :].astype(F32) for b, rows in where]
    p = _rwkv_pre(load(r_ref), load(lw_ref), load(k_ref), load(v_ref), load(kk_ref), load(b_ref),
                  tri, h64, strict, causal)
    s = [s_sc[b] for b in range(nb)]
    for j in range(n_chunks):
        items = list(range(j * nb, (j + 1) * nb))
        y, s = _rwkv_post(p, items, s, h64)
        for b in range(nb):
            y_ref[b, j * c:(j + 1) * c, :] = y[b]
    for b in range(nb):
        s_sc[b] = s[b]


def _recur_kernel(qk_ref, gv_ref, glo_ref, wgk_ref, bgk_ref, r_ref, lw_ref, k_ref, v_ref, kk_ref, b_ref,
                  tri_ref, h32_ref, h64_ref, strict_ref, causal_ref, og_ref, y_ref, st_sc, s_sc, *, n_chunks):
    _rwkv_kernel(r_ref, lw_ref, k_ref, v_ref, kk_ref, b_ref, tri_ref, h64_ref, strict_ref, causal_ref,
                 y_ref, s_sc, n_chunks=n_chunks)
    _gla_kernel(qk_ref, gv_ref, glo_ref, wgk_ref, bgk_ref, tri_ref, h32_ref, h64_ref, causal_ref,
                og_ref, st_sc, n_chunks=n_chunks)


def _recur(pb, wgk, bgk, r, lw, k2, v, kk, bvec, masks, *, batch, seq, tc):
    t = pb.shape[0]
    const = lambda s: (0, 0)
    pb3 = pb.reshape(batch, seq, pb.shape[1])
    seqs = [a.reshape(batch, seq, RWKV_WIDTH) for a in (r, lw, k2, v, kk, bvec)]
    spec = pl.BlockSpec((batch, tc, RWKV_WIDTH), lambda s: (0, s, 0))
    mask_names = ("tri", "head32", "head64", "strict", "causal")
    og, y = pl.pallas_call(
        functools.partial(_recur_kernel, n_chunks=tc // CHUNK),
        grid=(seq // tc,),
        in_specs=[pl.BlockSpec((batch, tc, 2 * GLA_QK_WIDTH), lambda s: (0, s, 0)),
                  pl.BlockSpec((batch, tc, GLA_WIDTH), lambda s: (0, s, 1)),
                  pl.BlockSpec((batch, tc, LANES), lambda s: (0, s, 6)),
                  pl.BlockSpec(wgk.shape, const),
                  pl.BlockSpec(bgk.shape, const)]
                 + [spec] * 6 + [pl.BlockSpec(masks[n].shape, const) for n in mask_names],
        out_specs=[pl.BlockSpec((batch, tc, GLA_WIDTH), lambda s: (0, s, 0)), spec],
        out_shape=[jax.ShapeDtypeStruct((batch, seq, GLA_WIDTH), F32),
                   jax.ShapeDtypeStruct((batch, seq, RWKV_WIDTH), F32)],
        scratch_shapes=[pltpu.VMEM((batch, GLA_WIDTH, GLA_QK_WIDTH), F32),
                        pltpu.VMEM((batch, RWKV_WIDTH, RWKV_WIDTH), F32)],
        compiler_params=_params("arbitrary"),
        name="recur",
    )(pb3, pb3, pb3, wgk, bgk, *seqs, *[masks[n] for n in mask_names])
    return og.reshape(t, GLA_WIDTH), y.reshape(t, RWKV_WIDTH)


def _mix_out_body(x, oa_ref, ob_ref, gout_ref, yc_ref, bonus_ref, gate_ref, na_ref, nb_ref,
                  lng_ref, lnb_ref, grp_ref, wa_ref, wb_ref, wc_ref):
    grp = grp_ref[...]
    ya = _rms(oa_ref[...].astype(F32), na_ref[...])
    ob = ob_ref[...]
    ms = _exact_right(ob * ob, grp) * (1.0 / GLA_DV)
    gout = gout_ref[...].astype(F32)
    yb = ob * lax.rsqrt(ms + NORM_EPS) * nb_ref[...] * (gout * _sigmoid(gout))
    y = yc_ref[...]
    mean = _exact_right(y, grp) * (1.0 / RWKV_N)
    d = y - mean
    var = _exact_right(d * d, grp) * (1.0 / RWKV_N)
    yc = (d * lax.rsqrt(var + RWKV_LN_EPS) * lng_ref[...] + lnb_ref[...] + bonus_ref[...].astype(F32)) * gate_ref[...].astype(F32)
    return x + _dot(_bf(ya), wa_ref[...]) + _dot(_bf(yb), wb_ref[...]) + _dot(_bf(yc), wc_ref[...])


def _shift_rows(y, carry):
    return jnp.concatenate([carry[SUBLANES - 1:SUBLANES, :], y[:y.shape[0] - 1, :]], axis=0)


def _causal_conv3(u, prev, cw, cb):
    c0, c1, c2 = cw[0:1, :], cw[1:2, :], cw[2:3, :]
    z_prev = c0 * prev
    w_prev = c1 * prev + pltpu.roll(z_prev, 1, 0)
    w = c1 * u + _shift_rows(c0 * u, z_prev)
    return c2 * u + cb + _shift_rows(w, w_prev)


def _mix_ffn_kernel(x_ref, oa_ref, ob_ref, gout_ref, yc_ref, bonus_ref, gate_ref, na_ref, nb_ref, lng_ref, lnb_ref,
                    grp_ref, wa_ref, wb_ref, wc_ref, g_ref, wu_ref, cw_ref, cb_ref, wd_ref, fg_ref,
                    o_ref, cg_sc, cv_sc, *, d_ff, tf, final_norm):
    x = _mix_out_body(x_ref[...], oa_ref, ob_ref, gout_ref, yc_ref, bonus_ref, gate_ref, na_ref, nb_ref,
                      lng_ref, lnb_ref, grp_ref, wa_ref, wb_ref, wc_ref)
    h = _bf(_rms(x, g_ref[...]))
    ts = h.shape[0]
    first = pl.program_id(1) == 0
    n_f = d_ff // tf

    def up(c):
        return (_dot(h, wu_ref[:, c * tf:(c + 1) * tf]),
                _dot(h, wu_ref[:, d_ff + c * tf:d_ff + (c + 1) * tf]))

    acc = None
    nxt = up(0)
    for c in range(n_f):
        ug, uv = nxt
        if c + 1 < n_f:
            nxt = up(c + 1)
        gcol = slice(c * tf, (c + 1) * tf)
        vcol = slice(d_ff + c * tf, d_ff + (c + 1) * tf)
        gate = _causal_conv3(ug, jnp.where(first, 0.0, cg_sc[c]), cw_ref[:, gcol], cb_ref[:, gcol])
        val = _causal_conv3(uv, jnp.where(first, 0.0, cv_sc[c]), cw_ref[:, vcol], cb_ref[:, vcol])
        cg_sc[c] = ug[ts - SUBLANES:, :]
        cv_sc[c] = uv[ts - SUBLANES:, :]
        part = _dot(_bf(gate * _sigmoid(gate) * val), wd_ref[gcol, :])
        acc = part if acc is None else acc + part
    y = x + acc
    if final_norm:
        y = _rms(y, fg_ref[...])
    o_ref[...] = y


def _mix_ffn(x, mixers, mix_params, g, w_up, conv_w, conv_b, w_down, final_g, *, batch, seq, ts, tf, final_norm):
    t, d = x.shape
    d_ff = w_down.shape[0]
    nsb = seq // ts
    const = lambda b, s: (0, 0)
    rows = lambda width, col=0: pl.BlockSpec((ts, width), lambda b, s: (b * nsb + s, col))
    resident = lambda a: pl.BlockSpec(a.shape, const, pipeline_mode=pl.Buffered(1))
    return pl.pallas_call(
        functools.partial(_mix_ffn_kernel, d_ff=d_ff, tf=tf, final_norm=final_norm),
        grid=(batch, nsb),
        in_specs=[rows(d), rows(MLA_WIDTH), rows(GLA_WIDTH),
                  rows(GLA_WIDTH, 2),
                  rows(RWKV_WIDTH), rows(RWKV_WIDTH), rows(RWKV_WIDTH)]
                 + [resident(a) for a in mix_params]
                 + [pl.BlockSpec((1, d), const),
                    resident(w_up), resident(conv_w), resident(conv_b), resident(w_down),
                    pl.BlockSpec((1, d), const)],
        out_specs=rows(d),
        out_shape=jax.ShapeDtypeStruct((t, d), F32),
        scratch_shapes=[pltpu.VMEM((d_ff // tf, SUBLANES, tf), F32),
                        pltpu.VMEM((d_ff // tf, SUBLANES, tf), F32)],
        compiler_params=_params("parallel", "arbitrary"),
        name="mix_ffn",
    )(x, *mixers, *mix_params, g, w_up, conv_w, conv_b, w_down, final_g)


def _rot_cols(w):
    half = w.shape[-1] // 2
    return jnp.concatenate([-w[..., half:], w[..., :half]], axis=-1)


def _regroup_w_in(w_in):
    d = w_in.shape[0]
    a0 = 0
    b0 = MLA_COLS
    c0 = MLA_COLS + GLA_COLS
    k_pe = w_in[:, MLA_Q_LORA + MLA_KV_LORA:MLA_COLS]
    gq = w_in[:, b0:b0 + GLA_QK_WIDTH]
    gk = w_in[:, b0 + GLA_QK_WIDTH:b0 + 2 * GLA_QK_WIDTH]
    gv = w_in[:, b0 + 2 * GLA_QK_WIDTH:b0 + 2 * GLA_QK_WIDTH + GLA_WIDTH]
    glo = w_in[:, b0 + 2 * GLA_QK_WIDTH + GLA_WIDTH:b0 + 2 * GLA_QK_WIDTH + GLA_WIDTH + GLA_GATE_RANK]
    gout = w_in[:, b0 + 2 * GLA_QK_WIDTH + GLA_WIDTH + GLA_GATE_RANK:c0]
    pad = jnp.zeros((d, LANES - GLA_GATE_RANK), w_in.dtype)
    cols = [w_in[:, a0:MLA_COLS], _rot_cols(k_pe), gq, gk, gv, gout, glo, pad, w_in[:, c0:]]
    return _bf(jnp.concatenate(cols, axis=1))


def _regroup_w_uq(w_uq):
    r = w_uq.shape[0]
    w = w_uq.reshape(r, MLA_HEADS, MLA_QK)
    nope = w[:, :, :MLA_NOPE]
    pe = w[:, :, MLA_NOPE:]
    return _bf(jnp.concatenate([nope, pe, _rot_cols(pe)], axis=-1).reshape(r, MLA_HEADS * QK_HEAD_W))


def _regroup_w_ukv(w_ukv):
    r = w_ukv.shape[0]
    w = w_ukv.reshape(r, MLA_HEADS, MLA_NOPE + MLA_V)
    k_nope = w[:, :, :MLA_NOPE].reshape(r, MLA_HEADS * MLA_NOPE)
    v = w[:, :, MLA_NOPE:].reshape(r, MLA_HEADS * MLA_V)
    return _bf(jnp.concatenate([k_nope, v], axis=1))


def _rope_table(seq):
    inv = 1.0 / (ROPE_THETA ** (jnp.arange(0, MLA_ROPE, 2, dtype=F32) / MLA_ROPE))
    ang = jnp.arange(seq, dtype=F32)[:, None] * inv[None, :]
    cos, sin = jnp.cos(ang), jnp.sin(ang)
    return jnp.concatenate([cos, cos, sin, sin], axis=1)


def _row(v):
    return v.reshape(1, -1).astype(F32)


def _pad_rows(w, top, total):
    return jnp.concatenate([jnp.zeros((top, w.shape[1]), w.dtype), w,
                            jnp.zeros((total - top - w.shape[0], w.shape[1]), w.dtype)], axis=0)


def _tiles(seq):
    return {
        "tm": min(512, seq),
        "tk": min(512, seq // 2),
        "tc": min(256, seq),
        "ts": min(512, seq),
        "tf": 256,
    }


def kernel(x, ln1_g, w_in, mla_q_norm_g, mla_w_uq, mla_kv_norm_g, mla_w_ukv, mla_out_norm_g, gla_w_gk, gla_b_gk, gla_norm_g, rwkv_mu, rwkv_w0, rwkv_w2, rwkv_a0, rwkv_a2, rwkv_g2, rwkv_k_k, rwkv_k_a, rwkv_r_k, rwkv_ln_g, rwkv_ln_b, w_out, ln2_g, ffn_w_up, ffn_conv_w, ffn_conv_b, ffn_w_down, final_g):
    batch, seq, d_model = x.shape
    depth = w_in.shape[0]
    tl = _tiles(seq)
    masks = _mask_inputs()
    grp = masks["group64"]
    rope = _rope_table(seq)
    xt = x.reshape(batch * seq, d_model)
    for l in range(depth):
        w2p = _bf(_pad_rows(rwkv_w2[l], 0, LANES))
        a2p = _bf(_pad_rows(rwkv_a2[l], RWKV_DECAY_RANK, LANES))
        mla = (_row(mla_q_norm_g[l]), _row(mla_kv_norm_g[l]), _regroup_w_uq(mla_w_uq[l]),
               _regroup_w_ukv(mla_w_ukv[l]))
        rwkv = (_row(rwkv_mu[l]), _row(rwkv_w0[l]), w2p, _row(rwkv_a0[l]), a2p, _bf(rwkv_g2[l]),
                _row(rwkv_k_k[l]), _row(rwkv_k_a[l]), _row(rwkv_r_k[l]))
        q, k, v, pb, r, lw, k2, vv, kk, bvec, gate, bonus = _proj(
            xt, _row(ln1_g[l]), _regroup_w_in(w_in[l]), mla, rwkv, rope, grp, tm=tl["tm"], seq=seq)
        o_mla = _flash(q, k, v, batch=batch, seq=seq, tk=tl["tk"])
        wgk = _bf(_pad_rows(gla_w_gk[l], 0, LANES))
        o_gla, y_rwkv = _recur(pb, wgk, _row(gla_b_gk[l]), r, lw, k2, vv, kk, bvec, masks,
                               batch=batch, seq=seq, tc=tl["tc"])
        wo = _bf(w_out[l])
        mix_params = (_row(mla_out_norm_g[l]), _row(jnp.tile(gla_norm_g[l], GLA_HEADS)), _row(rwkv_ln_g[l]),
                      _row(rwkv_ln_b[l]), grp, wo[:MLA_WIDTH], wo[MLA_WIDTH:MLA_WIDTH + GLA_WIDTH],
                      wo[MLA_WIDTH + GLA_WIDTH:])
        xt = _mix_ffn(xt, (o_mla, o_gla, pb, y_rwkv, bonus, gate), mix_params, _row(ln2_g[l]), _bf(ffn_w_up[l]),
                      ffn_conv_w[l].astype(F32), _row(ffn_conv_b[l]), _bf(ffn_w_down[l]), _row(final_g),
                      batch=batch, seq=seq, ts=tl["ts"], tf=tl["tf"], final_norm=(l == depth - 1))
    return xt.reshape(batch, seq, d_model)
```

```python
import functools
import math

import numpy as np
import jax
import jax.numpy as jnp
from jax import lax
from jax.experimental import pallas as pl
from jax.experimental.pallas import tpu as pltpu

F32 = jnp.float32
BF16 = jnp.bfloat16

MLA_HEADS = 4
MLA_NOPE = 128
MLA_ROPE = 64
MLA_V = 128
MLA_QK = MLA_NOPE + MLA_ROPE
MLA_Q_LORA = 384
MLA_KV_LORA = 256
MLA_WIDTH = MLA_HEADS * MLA_V
ROPE_THETA = 10000.0
GLA_HEADS = 4
GLA_DK = 32
GLA_DV = 64
GLA_QK_WIDTH = GLA_HEADS * GLA_DK
GLA_WIDTH = GLA_HEADS * GLA_DV
GLA_GATE_RANK = 16
GLA_GATE_NORM = 16.0
RWKV_HEADS = 4
RWKV_N = 64
RWKV_WIDTH = RWKV_HEADS * RWKV_N
RWKV_DECAY_RANK = 64
RWKV_A_RANK = 64
RWKV_GATE_RANK = 128
RWKV_LN_EPS = 64e-5
MLA_COLS = MLA_Q_LORA + MLA_KV_LORA + MLA_ROPE
GLA_COLS = 2 * GLA_QK_WIDTH + GLA_WIDTH + GLA_GATE_RANK + GLA_WIDTH
RWKV_COLS = 3 * RWKV_WIDTH + RWKV_DECAY_RANK + RWKV_A_RANK + RWKV_GATE_RANK
NORM_EPS = 1e-6
CONV_WIDTH = 3

LANES = 128
SUBLANES = 8
VMEM_LIMIT = 56 * 1024 * 1024

PA_W = MLA_Q_LORA + MLA_KV_LORA + 2 * MLA_ROPE
PB_W = 2 * GLA_QK_WIDTH + 2 * GLA_WIDTH + LANES
PC_W = RWKV_COLS
QK_HEAD_W = 2 * LANES

LOG2E = math.log2(math.e)

CHUNK = 64


def _dot(a, b):
    return lax.dot_general(a, b, (((1,), (0,)), ((), ())), preferred_element_type=F32)


def _dot_nt(a, b):
    return lax.dot_general(a, b, (((1,), (1,)), ((), ())), preferred_element_type=F32)


def _dot_tn(a, b):
    return lax.dot_general(a, b, (((0,), (0,)), ((), ())), preferred_element_type=F32)


def _bf(x):
    return x.astype(BF16)


def _split_terms(x, n):
    terms = []
    rem = x
    for _ in range(n):
        t = rem.astype(BF16)
        terms.append(t)
        rem = rem - t.astype(F32)
    return terms


def _exact_left(m, x, n=3):
    out = None
    for t in _split_terms(x, n):
        y = _dot(m, t)
        out = y if out is None else out + y
    return out


def _exact_right(x, m, n=2):
    out = None
    for t in _split_terms(x, n):
        y = _dot(t, m)
        out = y if out is None else out + y
    return out


def _rms(x, g, eps=NORM_EPS):
    return x * lax.rsqrt(jnp.mean(x * x, axis=-1, keepdims=True) + eps) * g


def _sigmoid(x):
    return 1.0 / (1.0 + jnp.exp(-x))


def _softplus(x):
    return jnp.maximum(x, 0.0) + jnp.log(1.0 + jnp.exp(-jnp.abs(x)))


def _params(*sem):
    return pltpu.CompilerParams(dimension_semantics=sem, vmem_limit_bytes=VMEM_LIMIT)


def _proj_kernel(x_ref, g_ref, w_ref, gq_ref, gkv_ref, wq_ref, wkv_ref, rope_ref, mu_ref, w0_ref, w2_ref,
                 a0_ref, a2_ref, g2_ref, kk_ref, ka_ref, rk_ref, grp_ref,
                 q_out, k_out, v_out, pb_out, r_out, lw_out, kr_out, vr_out, kkr_out, b_out, gate_out, bonus_out,
                 carry_sc, *, blocks_per_seq, n_step):
    h = _bf(_rms(x_ref[...], g_ref[...]))

    def group(base, width):
        cols = [_dot(h, w_ref[:, base + c:base + min(c + n_step, width)]) for c in range(0, width, n_step)]
        return cols[0] if len(cols) == 1 else jnp.concatenate(cols, axis=1)

    pa = group(0, PA_W)
    pc = group(PA_W + PB_W, PC_W)
    pb_out[...] = _bf(group(PA_W, PB_W))
    _mla_prep_body(pa, gq_ref, gkv_ref, wq_ref, wkv_ref, rope_ref, q_out, k_out, v_out)
    first = (pl.program_id(0) % blocks_per_seq) == 0
    last_prev = jnp.where(first, 0.0, carry_sc[SUBLANES - 1:SUBLANES, :])
    carry_sc[...] = pc[pc.shape[0] - SUBLANES:, :]
    _rwkv_prep_body(pc, last_prev, mu_ref, w0_ref, w2_ref, a0_ref, a2_ref, g2_ref, kk_ref, ka_ref, rk_ref, grp_ref,
                    r_out, lw_out, kr_out, vr_out, kkr_out, b_out, gate_out, bonus_out)


def _proj(x, g, w, mla, rwkv, rope, grp, *, tm, seq):
    t, d = x.shape
    const = lambda i: (0, 0)
    resident = lambda a: pl.BlockSpec(a.shape, const, pipeline_mode=pl.Buffered(1))
    row = lambda width: pl.BlockSpec((tm, width), lambda i: (i, 0))
    sds = lambda width, dt: jax.ShapeDtypeStruct((t, width), dt)
    qk_w = MLA_HEADS * QK_HEAD_W
    n_seq_blocks = seq // tm
    return pl.pallas_call(
        functools.partial(_proj_kernel, blocks_per_seq=n_seq_blocks, n_step=512),
        grid=(t // tm,),
        in_specs=[row(d), pl.BlockSpec((1, d), const), resident(w)]
                 + [resident(a) for a in mla]
                 + [pl.BlockSpec((tm, LANES), lambda i: (i % n_seq_blocks, 0))]
                 + [resident(a) for a in rwkv] + [resident(grp)],
        out_specs=[row(qk_w), row(qk_w), row(2 * MLA_WIDTH), row(PB_W)] + [row(RWKV_WIDTH)] * 8,
        out_shape=[sds(qk_w, BF16), sds(qk_w, BF16), sds(2 * MLA_WIDTH, BF16), sds(PB_W, BF16),
                   sds(RWKV_WIDTH, BF16), sds(RWKV_WIDTH, F32)] + [sds(RWKV_WIDTH, BF16)] * 6,
        scratch_shapes=[pltpu.VMEM((SUBLANES, PC_W), F32)],
        compiler_params=_params("arbitrary"),
        name="proj",
    )(x, g, w, *mla, rope, *rwkv, grp)


def _mla_prep_body(pa, gq_ref, gkv_ref, wq_ref, wkv_ref, rope_ref, q_ref, k_ref, v_ref):
    rope = rope_ref[...]
    cq = _bf(_rms(pa[:, :MLA_Q_LORA], gq_ref[...]))
    q = _dot(cq, wq_ref[...]) * (MLA_QK ** -0.5 * LOG2E)
    for h in range(MLA_HEADS):
        b = h * QK_HEAD_W
        q_ref[:, b:b + LANES] = _bf(q[:, b:b + LANES])
        q_ref[:, b + LANES:b + QK_HEAD_W] = _bf(q[:, b + LANES:b + QK_HEAD_W] * rope)
    ckv = _bf(_rms(pa[:, MLA_Q_LORA:MLA_Q_LORA + MLA_KV_LORA], gkv_ref[...]))
    kv = _dot(ckv, wkv_ref[...])
    kr = pa[:, MLA_Q_LORA + MLA_KV_LORA:] * rope
    kr = _bf(kr + pltpu.roll(kr, MLA_ROPE, 1))
    for h in range(MLA_HEADS):
        b = h * QK_HEAD_W
        k_ref[:, b:b + LANES] = _bf(kv[:, h * MLA_NOPE:(h + 1) * MLA_NOPE])
        k_ref[:, b + LANES:b + QK_HEAD_W] = kr
    ones = jnp.ones((pa.shape[0], MLA_V), BF16)
    for h in range(MLA_HEADS):
        b = h * 2 * MLA_V
        v_ref[:, b:b + MLA_V] = _bf(kv[:, (MLA_HEADS + h) * MLA_NOPE:(MLA_HEADS + h + 1) * MLA_NOPE])
        v_ref[:, b + MLA_V:b + 2 * MLA_V] = ones


def _flash_kernel(q_ref, k_ref, v_ref, o_ref, m_sc, acc_sc, s_sc, *, tk, nq):
    qi = pl.program_id(2)
    m_sc[...] = jnp.full(m_sc.shape, -jnp.inf, F32)
    acc_sc[...] = jnp.zeros(acc_sc.shape, F32)

    def kv_rows(j):
        return pl.ds(pl.multiple_of(j * tk, tk), tk)

    def scores(tile, sub, j):
        q_rows = pl.ds(pl.multiple_of((2 * tile + sub) * tk, tk), tk)
        return _dot_nt(q_ref[q_rows, :], k_ref[kv_rows(j), :])

    def accumulate(sub, j, s, masked):
        rows = slice(sub * tk, (sub + 1) * tk)
        if masked:
            row = lax.broadcasted_iota(jnp.int32, s.shape, 0)
            col = lax.broadcasted_iota(jnp.int32, s.shape, 1)
            s = jnp.where(row >= col, s, -jnp.inf)
        m_prev = m_sc[rows, :]
        m_next = jnp.maximum(m_prev, jnp.max(s, axis=1, keepdims=True))
        alpha = jnp.exp2(m_prev - m_next)
        p = jnp.exp2(s - jnp.concatenate([m_next] * (tk // LANES), axis=1))
        acc_sc[rows, :] = (acc_sc[rows, :] * jnp.concatenate([alpha, alpha], axis=1)
                           + _dot(_bf(p), v_ref[kv_rows(j), :]))
        m_sc[rows, :] = m_next

    @pl.when(qi == 0)
    def _():
        for sub in range(2):
            s_sc[0, sub] = scores(0, sub, 0)

    def body(jj, carry):
        j = 2 * jj
        for slot in range(2):
            for sub in range(2):
                s = s_sc[slot, sub]
                s_sc[1 - slot, sub] = scores(qi, sub, j + slot + 1)
                accumulate(sub, j + slot, s, False)
        return carry

    half = lax.shift_right_logical(qi, 1)
    lax.fori_loop(0, half, lambda t, carry: body(2 * t + 1, body(2 * t, carry)), 0)
    lax.fori_loop(2 * half, qi, body, 0)
    j = 2 * qi
    s_last = scores(qi, 1, j + 1)
    nxt = jnp.minimum(qi + 1, nq - 1)
    s_next = [scores(nxt, sub, 0) for sub in range(2)]
    accumulate(0, j, s_sc[0, 0], True)
    accumulate(1, j, s_sc[0, 1], False)
    accumulate(1, j + 1, s_last, True)
    for sub in range(2):
        s_sc[0, sub] = s_next[sub]
    acc = acc_sc[...]
    o_ref[...] = _bf(acc[:, :MLA_V] / acc[:, MLA_V:])


def _flash(q, k, v1, *, batch, seq, tk):
    tq = 2 * tk
    nq = seq // tq
    return pl.pallas_call(
        functools.partial(_flash_kernel, tk=tk, nq=nq),
        grid=(batch, MLA_HEADS, nq),
        in_specs=[pl.BlockSpec((seq, QK_HEAD_W), lambda b, h, i: (b, h)),
                  pl.BlockSpec((seq, QK_HEAD_W), lambda b, h, i: (b, h)),
                  pl.BlockSpec((seq, 2 * MLA_V), lambda b, h, i: (b, h))],
        out_specs=pl.BlockSpec((tq, MLA_V), lambda b, h, i: (b * nq + i, h)),
        out_shape=jax.ShapeDtypeStruct((batch * seq, MLA_WIDTH), BF16),
        scratch_shapes=[pltpu.VMEM((tq, LANES), F32),
                        pltpu.VMEM((tq, 2 * MLA_V), F32),
                        pltpu.VMEM((2, 2, tk, tk), F32)],
        compiler_params=_params("parallel", "parallel", "arbitrary"),
        name="flash",
    )(q, k, v1)


def _np_masks():
    c = CHUNK
    r4 = np.arange(4 * c)
    tri = (np.arange(c)[:, None] >= np.arange(c)[None, :])
    m = {
        "tri": tri,
        "head64": (r4[:, None] // c) == (np.arange(RWKV_WIDTH)[None, :] // RWKV_N),
        "head32": (r4[:, None] // c) == (np.arange(GLA_QK_WIDTH)[None, :] // GLA_DK),
        "strict": np.arange(c)[:, None] > (r4[None, :] % c),
        "causal": np.arange(c)[:, None] >= (r4[None, :] % c),
    }
    return m


def _mask_inputs():
    m = _np_masks()
    return {
        "tri": jnp.asarray(m["tri"], BF16),
        "head64": jnp.asarray(m["head64"], F32),
        "head32": jnp.asarray(m["head32"], F32),
        "strict": jnp.asarray(m["strict"], F32),
        "causal": jnp.asarray(m["causal"], F32),
        "group64": jnp.asarray(m["head64"], BF16),
    }


def _stack(x, head_mask):
    return jnp.concatenate([x] * 4, axis=0) * head_mask


def _gla_kernel(qk_ref, v_ref, glo_ref, wgk_ref, bgk_ref, tri_ref, h32_ref, h64_ref, causal_ref,
                o_ref, st_sc, *, n_chunks):
    @pl.when(pl.program_id(0) == 0)
    def _():
        st_sc[...] = jnp.zeros(st_sc.shape, F32)

    c = CHUNK
    tri = tri_ref[...]
    h32 = h32_ref[...]
    h64 = h64_ref[...]
    causal = causal_ref[...] != 0
    nb = qk_ref.shape[0]
    items = [(b, slice(j * c, (j + 1) * c)) for j in range(n_chunks) for b in range(nb)]
    pre = [_dot(glo_ref[b], wgk_ref[...]) + bgk_ref[...] for b in range(nb)]
    gk = [-_softplus(-pre[b][rows, :]) / GLA_GATE_NORM for b, rows in items]
    q = [qk_ref[b, rows, :GLA_QK_WIDTH].astype(F32) for b, rows in items]
    k = [qk_ref[b, rows, GLA_QK_WIDTH:].astype(F32) for b, rows in items]
    v = [v_ref[b, rows, :] for b, rows in items]
    bc = _each(lambda x: _exact_left(tri, x), gk)
    b_last = _each(lambda x: x[c - 1:c, :], bc)
    q_t = _each(lambda q_, b_: q_ * (GLA_DK ** -0.5) * jnp.exp(b_), q, bc)
    k_t = _each(lambda k_, b_: k_ * jnp.exp(-b_), k, bc)
    k_end = _each(lambda k_, bl, b_: k_ * jnp.exp(bl - b_), k, b_last, bc)
    a = _each(lambda q_, k_: jnp.where(causal, _dot_nt(_bf(q_), _bf(_stack(k_, h32))), 0.0), q_t, k_t)
    o_intra = _each(lambda a_, v_: _dot(_bf(a_), _bf(_stack(v_, h64))), a, v)
    upd = _each(lambda v_, ke: _dot_tn(_bf(v_), _bf(ke)) * h32, v, k_end)
    decay = _each(jnp.exp, b_last)
    st = [st_sc[b] for b in range(nb)]
    for i, (b, rows) in enumerate(items):
        o_ref[b, rows, :] = o_intra[i] + _dot_nt(_bf(q_t[i]), _bf(st[b]))
        st[b] = st[b] * decay[i] + upd[i]
    for b in range(nb):
        st_sc[b] = st[b]


def _rwkv_prep_body(xc, last_prev, mu_ref, w0_ref, w2_ref, a0_ref, a2_ref, g2_ref, kk_ref, ka_ref,
                    rk_ref, grp_ref, r_out, lw_out, k_out, v_out, kk_out, b_out, g_out, bonus_out):
    row = lax.broadcasted_iota(jnp.int32, xc.shape, 0)
    prev = jnp.where(row == 0, last_prev, pltpu.roll(xc, 1, 0))
    xm = xc + (prev - xc) * mu_ref[...]
    w3 = RWKV_WIDTH
    r = xm[:, 0:w3]
    k = xm[:, w3:2 * w3]
    v = xm[:, 2 * w3:3 * w3]
    wa = xm[:, 3 * w3:3 * w3 + LANES]
    g_lo = xm[:, 3 * w3 + LANES:]
    w = -_softplus(-(w0_ref[...] + _dot(_bf(jnp.tanh(wa)), w2_ref[...]))) - 0.5
    a = _sigmoid(a0_ref[...] + _dot(_bf(wa), a2_ref[...]))
    g = _dot(_bf(_sigmoid(g_lo)), g2_ref[...])
    grp = grp_ref[...]
    kk = k * kk_ref[...]
    norm = jnp.sqrt(_exact_right(kk * kk, grp))
    kk = kk / jnp.maximum(norm, 1e-12)
    k2 = k * (1.0 + (a - 1.0) * ka_ref[...])
    r_out[...] = _bf(r)
    lw_out[...] = -jnp.exp(w)
    k_out[...] = _bf(k2)
    v_out[...] = _bf(v)
    kk_out[...] = _bf(kk)
    b_out[...] = _bf(kk * a)
    g_out[...] = _bf(g)
    bonus_out[...] = _bf(_exact_right(r * k2 * rk_ref[...], grp) * v)


def _each(fn, *lists):
    return [fn(*xs) for xs in zip(*lists)]


def _rwkv_pre(r, lw, k2, v, kk, bvec, tri, h64, strict, causal):
    c = CHUNK
    g = _each(lambda x: _exact_left(tri, x), lw)
    g_last = _each(lambda x: x[c - 1:c, :], g)
    e_neg = _each(lambda x: jnp.exp(-x), g)
    a_t = _each(lambda kk_, g_, lw_: -kk_ * jnp.exp(g_ - lw_), kk, g, lw)
    r_t = _each(lambda r_, g_: r_ * jnp.exp(g_), r, g)
    b_t = _each(jnp.multiply, bvec, e_neg)
    k_t = _each(jnp.multiply, k2, e_neg)
    e_end = _each(lambda gl, g_: jnp.exp(gl - g_), g_last, g)
    b_end = _each(jnp.multiply, bvec, e_end)
    k_end = _each(jnp.multiply, k2, e_end)

    h64b = _bf(h64)
    bd = lambda x: _stack(_bf(x), h64b)
    a_s = _each(bd, a_t)
    bk_s = _each(lambda b_, k_: jnp.concatenate([bd(b_), bd(k_)], axis=0), b_t, k_t)
    v_s = _each(bd, v)
    ar = _each(lambda a_, r_: _bf(jnp.concatenate([a_, r_], axis=0)), a_t, r_t)
    prod = _each(_dot_nt, ar, bk_s)
    l1 = _each(lambda x: jnp.where(strict, x[:c, :4 * c], 0.0), prod)
    a_ak = _each(lambda x: jnp.where(strict, x[:c, 4 * c:], 0.0), prod)
    a_rb = _each(lambda x: jnp.where(causal, x[c:, :4 * c], 0.0), prod)
    a_rk = _each(lambda x: jnp.where(causal, x[c:, 4 * c:], 0.0), prod)

    shape = l1[0].shape
    eye = (lax.broadcasted_iota(jnp.int32, shape, 0)
           == lax.broadcasted_iota(jnp.int32, shape, 1) % c).astype(F32)
    t_inv = _each(lambda x: eye + x, l1)
    lp = l1
    lp_bd = _each(bd, lp)
    for _ in range(5):
        lp = _each(lambda x, xb: _dot(_bf(x), xb), lp, lp_bd)
        lp_bd = _each(bd, lp)
        t_inv = _each(lambda t, pb: t + _dot(_bf(t), pb), t_inv, lp_bd)
    t_b = _each(_bf, t_inv)
    av_yv = _each(lambda a, ark, vs: _dot(_bf(jnp.concatenate([a, ark], axis=0)), vs), a_ak, a_rk, v_s)
    y_v = _each(lambda x: x[c:], av_yv)
    w_a = _each(_dot, t_b, a_s)
    u_v = _each(lambda t, x: _dot(t, bd(x[:c])), t_b, av_yv)

    bk_end = _each(lambda b_, k_: _bf(jnp.concatenate([b_, k_], axis=0)), b_end, k_end)
    decay = _each(jnp.exp, g_last)
    wr = _each(lambda w, r_: _bf(jnp.concatenate([w, r_], axis=0)), w_a, r_t)
    return {"wr": wr, "u_v": u_v, "a_rb": _each(_bf, a_rb), "y_v": y_v, "v": v, "bk_end": bk_end, "decay": decay}


def _rwkv_post(p, items, s, h64):
    c = CHUNK
    pick = lambda name: [p[name][i] for i in items]
    ur = _each(lambda wr, s_: _dot_nt(wr, _bf(s_)), pick("wr"), s)
    u = _each(lambda x, uv_: x[:c] + uv_, ur, pick("u_v"))
    h64b = _bf(h64)
    y = _each(lambda x, arb, u_, yv: x[c:] + _dot(arb, _stack(_bf(u_), h64b)) + yv,
              ur, pick("a_rb"), u, pick("y_v"))
    uv = _each(lambda u_, v_: _bf(jnp.concatenate([u_, v_], axis=0)), u, pick("v"))
    s_new = _each(lambda s_, d, x, z: s_ * d + _dot_tn(x, z) * h64, s, pick("decay"), uv, pick("bk_end"))
    return y, s_new


def _rwkv_kernel(r_ref, lw_ref, k_ref, v_ref, kk_ref, b_ref, tri_ref, h64_ref, strict_ref, causal_ref,
                 y_ref, s_sc, *, n_chunks):
    @pl.when(pl.program_id(0) == 0)
    def _():
        s_sc[...] = jnp.zeros(s_sc.shape, F32)

    tri = tri_ref[...]
    h64 = h64_ref[...]
    strict = strict_ref[...] != 0
    causal = causal_ref[...] != 0
    nb = r_ref.shape[0]
    c = CHUNK
    where = [(b, slice(j * c, (j + 1) * c)) for j in range(n_chunks) for b in range(nb)]
    load = lambda ref: [ref[b, rows, :].astype(F32) for b, rows in where]
    p = _rwkv_pre(load(r_ref), load(lw_ref), load(k_ref), load(v_ref), load(kk_ref), load(b_ref),
                  tri, h64, strict, causal)
    s = [s_sc[b] for b in range(nb)]
    for j in range(n_chunks):
        items = list(range(j * nb, (j + 1) * nb))
        y, s = _rwkv_post(p, items, s, h64)
        for b in range(nb):
            y_ref[b, j * c:(j + 1) * c, :] = y[b]
    for b in range(nb):
        s_sc[b] = s[b]


def _recur_kernel(qk_ref, gv_ref, glo_ref, wgk_ref, bgk_ref, r_ref, lw_ref, k_ref, v_ref, kk_ref, b_ref,
                  tri_ref, h32_ref, h64_ref, strict_ref, causal_ref, og_ref, y_ref, st_sc, s_sc, *, n_chunks):
    _rwkv_kernel(r_ref, lw_ref, k_ref, v_ref, kk_ref, b_ref, tri_ref, h64_ref, strict_ref, causal_ref,
                 y_ref, s_sc, n_chunks=n_chunks)
    _gla_kernel(qk_ref, gv_ref, glo_ref, wgk_ref, bgk_ref, tri_ref, h32_ref, h64_ref, causal_ref,
                og_ref, st_sc, n_chunks=n_chunks)


def _recur(pb, wgk, bgk, r, lw, k2, v, kk, bvec, masks, *, batch, seq, tc):
    t = pb.shape[0]
    const = lambda s: (0, 0)
    pb3 = pb.reshape(batch, seq, pb.shape[1])
    seqs = [a.reshape(batch, seq, RWKV_WIDTH) for a in (r, lw, k2, v, kk, bvec)]
    spec = pl.BlockSpec((batch, tc, RWKV_WIDTH), lambda s: (0, s, 0))
    mask_names = ("tri", "head32", "head64", "strict", "causal")
    og, y = pl.pallas_call(
        functools.partial(_recur_kernel, n_chunks=tc // CHUNK),
        grid=(seq // tc,),
        in_specs=[pl.BlockSpec((batch, tc, 2 * GLA_QK_WIDTH), lambda s: (0, s, 0)),
                  pl.BlockSpec((batch, tc, GLA_WIDTH), lambda s: (0, s, 1)),
                  pl.BlockSpec((batch, tc, LANES), lambda s: (0, s, 6)),
                  pl.BlockSpec(wgk.shape, const),
                  pl.BlockSpec(bgk.shape, const)]
                 + [spec] * 6 + [pl.BlockSpec(masks[n].shape, const) for n in mask_names],
        out_specs=[pl.BlockSpec((batch, tc, GLA_WIDTH), lambda s: (0, s, 0)), spec],
        out_shape=[jax.ShapeDtypeStruct((batch, seq, GLA_WIDTH), F32),
                   jax.ShapeDtypeStruct((batch, seq, RWKV_WIDTH), F32)],
        scratch_shapes=[pltpu.VMEM((batch, GLA_WIDTH, GLA_QK_WIDTH), F32),
                        pltpu.VMEM((batch, RWKV_WIDTH, RWKV_WIDTH), F32)],
        compiler_params=_params("arbitrary"),
        name="recur",
    )(pb3, pb3, pb3, wgk, bgk, *seqs, *[masks[n] for n in mask_names])
    return og.reshape(t, GLA_WIDTH), y.reshape(t, RWKV_WIDTH)


def _mix_out_body(x, oa_ref, ob_ref, gout_ref, yc_ref, bonus_ref, gate_ref, na_ref, nb_ref,
                  lng_ref, lnb_ref, grp_ref, wa_ref, wb_ref, wc_ref):
    grp = grp_ref[...]
    ya = _rms(oa_ref[...].astype(F32), na_ref[...])
    ob = ob_ref[...]
    ms = _exact_right(ob * ob, grp) * (1.0 / GLA_DV)
    gout = gout_ref[...].astype(F32)
    yb = ob * lax.rsqrt(ms + NORM_EPS) * nb_ref[...] * (gout * _sigmoid(gout))
    y = yc_ref[...]
    mean = _exact_right(y, grp) * (1.0 / RWKV_N)
    d = y - mean
    var = _exact_right(d * d, grp) * (1.0 / RWKV_N)
    yc = (d * lax.rsqrt(var + RWKV_LN_EPS) * lng_ref[...] + lnb_ref[...] + bonus_ref[...].astype(F32)) * gate_ref[...].astype(F32)
    return x + _dot(_bf(ya), wa_ref[...]) + _dot(_bf(yb), wb_ref[...]) + _dot(_bf(yc), wc_ref[...])


def _shift_rows(y, carry):
    return jnp.concatenate([carry[SUBLANES - 1:SUBLANES, :], y[:y.shape[0] - 1, :]], axis=0)


def _causal_conv3(u, prev, cw, cb):
    c0, c1, c2 = cw[0:1, :], cw[1:2, :], cw[2:3, :]
    z_prev = c0 * prev
    w_prev = c1 * prev + pltpu.roll(z_prev, 1, 0)
    w = c1 * u + _shift_rows(c0 * u, z_prev)
    return c2 * u + cb + _shift_rows(w, w_prev)


def _mix_ffn_kernel(x_ref, oa_ref, ob_ref, gout_ref, yc_ref, bonus_ref, gate_ref, na_ref, nb_ref, lng_ref, lnb_ref,
                    grp_ref, wa_ref, wb_ref, wc_ref, g_ref, wu_ref, cw_ref, cb_ref, wd_ref, fg_ref,
                    o_ref, cg_sc, cv_sc, *, d_ff, tf, final_norm):
    x = _mix_out_body(x_ref[...], oa_ref, ob_ref, gout_ref, yc_ref, bonus_ref, gate_ref, na_ref, nb_ref,
                      lng_ref, lnb_ref, grp_ref, wa_ref, wb_ref, wc_ref)
    h = _bf(_rms(x, g_ref[...]))
    ts = h.shape[0]
    first = pl.program_id(1) == 0
    n_f = d_ff // tf

    def up(c):
        return (_dot(h, wu_ref[:, c * tf:(c + 1) * tf]),
                _dot(h, wu_ref[:, d_ff + c * tf:d_ff + (c + 1) * tf]))

    acc = None
    nxt = up(0)
    for c in range(n_f):
        ug, uv = nxt
        if c + 1 < n_f:
            nxt = up(c + 1)
        gcol = slice(c * tf, (c + 1) * tf)
        vcol = slice(d_ff + c * tf, d_ff + (c + 1) * tf)
        gate = _causal_conv3(ug, jnp.where(first, 0.0, cg_sc[c]), cw_ref[:, gcol], cb_ref[:, gcol])
        val = _causal_conv3(uv, jnp.where(first, 0.0, cv_sc[c]), cw_ref[:, vcol], cb_ref[:, vcol])
        cg_sc[c] = ug[ts - SUBLANES:, :]
        cv_sc[c] = uv[ts - SUBLANES:, :]
        part = _dot(_bf(gate * _sigmoid(gate) * val), wd_ref[gcol, :])
        acc = part if acc is None else acc + part
    y = x + acc
    if final_norm:
        y = _rms(y, fg_ref[...])
    o_ref[...] = y


def _mix_ffn(x, mixers, mix_params, g, w_up, conv_w, conv_b, w_down, final_g, *, batch, seq, ts, tf, final_norm):
    t, d = x.shape
    d_ff = w_down.shape[0]
    nsb = seq // ts
    const = lambda b, s: (0, 0)
    rows = lambda width, col=0: pl.BlockSpec((ts, width), lambda b, s: (b * nsb + s, col))
    resident = lambda a: pl.BlockSpec(a.shape, const, pipeline_mode=pl.Buffered(1))
    return pl.pallas_call(
        functools.partial(_mix_ffn_kernel, d_ff=d_ff, tf=tf, final_norm=final_norm),
        grid=(batch, nsb),
        in_specs=[rows(d), rows(MLA_WIDTH), rows(GLA_WIDTH),
                  rows(GLA_WIDTH, 2),
                  rows(RWKV_WIDTH), rows(RWKV_WIDTH), rows(RWKV_WIDTH)]
                 + [resident(a) for a in mix_params]
                 + [pl.BlockSpec((1, d), const),
                    resident(w_up), resident(conv_w), resident(conv_b), resident(w_down),
                    pl.BlockSpec((1, d), const)],
        out_specs=rows(d),
        out_shape=jax.ShapeDtypeStruct((t, d), F32),
        scratch_shapes=[pltpu.VMEM((d_ff // tf, SUBLANES, tf), F32),
                        pltpu.VMEM((d_ff // tf, SUBLANES, tf), F32)],
        compiler_params=_params("parallel", "arbitrary"),
        name="mix_ffn",
    )(x, *mixers, *mix_params, g, w_up, conv_w, conv_b, w_down, final_g)


def _rot_cols(w):
    half = w.shape[-1] // 2
    return jnp.concatenate([-w[..., half:], w[..., :half]], axis=-1)


def _regroup_w_in(w_in):
    d = w_in.shape[0]
    a0 = 0
    b0 = MLA_COLS
    c0 = MLA_COLS + GLA_COLS
    k_pe = w_in[:, MLA_Q_LORA + MLA_KV_LORA:MLA_COLS]
    gq = w_in[:, b0:b0 + GLA_QK_WIDTH]
    gk = w_in[:, b0 + GLA_QK_WIDTH:b0 + 2 * GLA_QK_WIDTH]
    gv = w_in[:, b0 + 2 * GLA_QK_WIDTH:b0 + 2 * GLA_QK_WIDTH + GLA_WIDTH]
    glo = w_in[:, b0 + 2 * GLA_QK_WIDTH + GLA_WIDTH:b0 + 2 * GLA_QK_WIDTH + GLA_WIDTH + GLA_GATE_RANK]
    gout = w_in[:, b0 + 2 * GLA_QK_WIDTH + GLA_WIDTH + GLA_GATE_RANK:c0]
    pad = jnp.zeros((d, LANES - GLA_GATE_RANK), w_in.dtype)
    cols = [w_in[:, a0:MLA_COLS], _rot_cols(k_pe), gq, gk, gv, gout, glo, pad, w_in[:, c0:]]
    return _bf(jnp.concatenate(cols, axis=1))


def _regroup_w_uq(w_uq):
    r = w_uq.shape[0]
    w = w_uq.reshape(r, MLA_HEADS, MLA_QK)
    nope = w[:, :, :MLA_NOPE]
    pe = w[:, :, MLA_NOPE:]
    return _bf(jnp.concatenate([nope, pe, _rot_cols(pe)], axis=-1).reshape(r, MLA_HEADS * QK_HEAD_W))


def _regroup_w_ukv(w_ukv):
    r = w_ukv.shape[0]
    w = w_ukv.reshape(r, MLA_HEADS, MLA_NOPE + MLA_V)
    k_nope = w[:, :, :MLA_NOPE].reshape(r, MLA_HEADS * MLA_NOPE)
    v = w[:, :, MLA_NOPE:].reshape(r, MLA_HEADS * MLA_V)
    return _bf(jnp.concatenate([k_nope, v], axis=1))


def _rope_table(seq):
    inv = 1.0 / (ROPE_THETA ** (jnp.arange(0, MLA_ROPE, 2, dtype=F32) / MLA_ROPE))
    ang = jnp.arange(seq, dtype=F32)[:, None] * inv[None, :]
    cos, sin = jnp.cos(ang), jnp.sin(ang)
    return jnp.concatenate([cos, cos, sin, sin], axis=1)


def _row(v):
    return v.reshape(1, -1).astype(F32)


def _pad_rows(w, top, total):
    return jnp.concatenate([jnp.zeros((top, w.shape[1]), w.dtype), w,
                            jnp.zeros((total - top - w.shape[0], w.shape[1]), w.dtype)], axis=0)


def _tiles(seq):
    return {
        "tm": min(512, seq),
        "tk": min(512, seq // 2),
        "tc": min(256, seq),
        "ts": min(512, seq),
        "tf": 256,
    }


def kernel(x, ln1_g, w_in, mla_q_norm_g, mla_w_uq, mla_kv_norm_g, mla_w_ukv, mla_out_norm_g, gla_w_gk, gla_b_gk, gla_norm_g, rwkv_mu, rwkv_w0, rwkv_w2, rwkv_a0, rwkv_a2, rwkv_g2, rwkv_k_k, rwkv_k_a, rwkv_r_k, rwkv_ln_g, rwkv_ln_b, w_out, ln2_g, ffn_w_up, ffn_conv_w, ffn_conv_b, ffn_w_down, final_g):
    batch, seq, d_model = x.shape
    depth = w_in.shape[0]
    tl = _tiles(seq)
    masks = _mask_inputs()
    grp = masks["group64"]
    rope = _rope_table(seq)
    xt = x.reshape(batch * seq, d_model)
    for l in range(depth):
        w2p = _bf(_pad_rows(rwkv_w2[l], 0, LANES))
        a2p = _bf(_pad_rows(rwkv_a2[l], RWKV_DECAY_RANK, LANES))
        mla = (_row(mla_q_norm_g[l]), _row(mla_kv_norm_g[l]), _regroup_w_uq(mla_w_uq[l]),
               _regroup_w_ukv(mla_w_ukv[l]))
        rwkv = (_row(rwkv_mu[l]), _row(rwkv_w0[l]), w2p, _row(rwkv_a0[l]), a2p, _bf(rwkv_g2[l]),
                _row(rwkv_k_k[l]), _row(rwkv_k_a[l]), _row(rwkv_r_k[l]))
        q, k, v, pb, r, lw, k2, vv, kk, bvec, gate, bonus = _proj(
            xt, _row(ln1_g[l]), _regroup_w_in(w_in[l]), mla, rwkv, rope, grp, tm=tl["tm"], seq=seq)
        o_mla = _flash(q, k, v, batch=batch, seq=seq, tk=tl["tk"])
        wgk = _bf(_pad_rows(gla_w_gk[l], 0, LANES))
        o_gla, y_rwkv = _recur(pb, wgk, _row(gla_b_gk[l]), r, lw, k2, vv, kk, bvec, masks,
                               batch=batch, seq=seq, tc=tl["tc"])
        wo = _bf(w_out[l])
        mix_params = (_row(mla_out_norm_g[l]), _row(jnp.tile(gla_norm_g[l], GLA_HEADS)), _row(rwkv_ln_g[l]),
                      _row(rwkv_ln_b[l]), grp, wo[:MLA_WIDTH], wo[MLA_WIDTH:MLA_WIDTH + GLA_WIDTH],
                      wo[MLA_WIDTH + GLA_WIDTH:])
        xt = _mix_ffn(xt, (o_mla, o_gla, pb, y_rwkv, bonus, gate), mix_params, _row(ln2_g[l]), _bf(ffn_w_up[l]),
                      ffn_conv_w[l].astype(F32), _row(ffn_conv_b[l]), _bf(ffn_w_down[l]), _row(final_g),
                      batch=batch, seq=seq, ts=tl["ts"], tf=tl["tf"], final_norm=(l == depth - 1))
    return xt.reshape(batch, seq, d_model)
```
